```python
import functools
import jax, jax.numpy as jnp
from jax import lax
import numpy as np

D_MODEL = 1024
BATCH = 2
SEQ = 8192
DEPTH = 1
DEC_BATCH = 128
DEC_SEQ = 1
PAST_LEN = 8192
PAGE_SIZE = 128

N_HEADS = 8
HEAD_DIM = D_MODEL // N_HEADS
ATTN_WIDTH = N_HEADS * HEAD_DIM
D_RNN = D_MODEL
N_RNN_BLOCKS = 8
RNN_BLOCK = D_RNN // N_RNN_BLOCKS
CONV_W = 4
LRU_C = 8.0
N_GROUPS = 4
EXPERTS_PER_GROUP = 8
N_EXPERTS = N_GROUPS * EXPERTS_PER_GROUP
TOP_K = 2
D_EXPERT = D_MODEL // 4
Q_BLOCK = 128
LN_EPS = 1e-5
ALPHA = (2.0 * DEPTH) ** 0.25
BETA = (8.0 * DEPTH) ** -0.25
N_IN = 2 * D_RNN + 3 * ATTN_WIDTH + N_HEADS + 2 * D_MODEL
F32 = jnp.float32

kernel_name = 'hybrid_rglru_fox_hmoe_step'


def layer_norm(x, g, b):
    xf = x.astype(F32)
    mu = jnp.mean(xf, -1, keepdims=True)
    var = jnp.mean(jnp.square(xf - mu), -1, keepdims=True)
    y = (xf - mu) * lax.rsqrt(var + LN_EPS)
    return (y * g.astype(F32) + b.astype(F32)).astype(x.dtype)


def adaln(c, w_mod, b_mod):
    m = jax.nn.silu(c) @ w_mod + b_mod
    return jnp.split(m[:, None, :], 6, axis=-1)


def causal_conv(xpad, w, b):
    T = xpad.shape[1] - (CONV_W - 1)
    y = b
    for tap in range(CONV_W):
        y = y + xpad[:, tap:tap + T] * w[tap]
    return y


def rg_lru(xc, w_rg, b_rg, w_ig, b_ig, lam, h_prev, is_start):
    B, T, _ = xc.shape
    xf = xc.astype(F32)
    xblk = xf.reshape(B, T, N_RNN_BLOCKS, RNN_BLOCK)
    r = jax.nn.sigmoid(jnp.einsum('btni,nij->btnj', xblk, w_rg.astype(F32)).reshape(B, T, D_RNN) + b_rg.astype(F32))
    i = jax.nn.sigmoid(jnp.einsum('btni,nij->btnj', xblk, w_ig.astype(F32)).reshape(B, T, D_RNN) + b_ig.astype(F32))
    log_a = -LRU_C * r * jax.nn.softplus(-lam.astype(F32))
    a = jnp.exp(log_a)
    mult = jnp.sqrt(-jnp.expm1(2.0 * log_a))
    if is_start:
        mult = mult.at[:, 0].set(1.0)
    bx = mult * i * xf

    def step(h, inp):
        a_t, b_t = inp
        h = a_t * h + b_t
        return h, h

    h_last, hs = lax.scan(step, h_prev.astype(F32), (jnp.swapaxes(a, 0, 1), jnp.swapaxes(bx, 0, 1)))
    return jnp.swapaxes(hs, 0, 1), h_last


def forgetting_attention_prompt(q, k, v, logf):
    B, T, H, Dh = q.shape
    scale = HEAD_DIM ** -0.5
    F = jnp.cumsum(logf, axis=1)
    nb = T // Q_BLOCK
    kpos = jnp.arange(T)
    Fk = jnp.transpose(F, (0, 2, 1))[:, :, None, :]
    qb = jnp.moveaxis(q.reshape(B, nb, Q_BLOCK, H, Dh), 1, 0)
    Fq = jnp.moveaxis(F.reshape(B, nb, Q_BLOCK, H), 1, 0)

    def block(args):
        q_blk, F_blk, b_idx = args
        qpos = b_idx * Q_BLOCK + jnp.arange(Q_BLOCK)
        s = jnp.einsum('bqhd,bkhd->bhqk', q_blk, k).astype(F32) * scale
        s = s + jnp.transpose(F_blk, (0, 2, 1))[..., None] - Fk
        s = jnp.where(qpos[:, None] >= kpos[None, :], s, -jnp.inf)
        p = jax.nn.softmax(s, axis=-1)
        return jnp.einsum('bhqk,bkhd->bqhd', p, v.astype(F32))

    out = lax.map(block, (qb, Fq, jnp.arange(nb)))
    return jnp.moveaxis(out, 0, 1).reshape(B, T, H, Dh)


def forgetting_attention_sample(q, k, v, logf, cache_k, cache_v, cache_logf, page_table, layer):
    DB, S, H, Dh = q.shape
    n_pages = page_table.shape[1]
    scale = HEAD_DIM ** -0.5
    lf_past = cache_logf[layer, page_table].astype(F32).reshape(DB, n_pages * PAGE_SIZE, H)
    suf = jnp.cumsum(lf_past[:, ::-1], axis=1)[:, ::-1] - lf_past
    cnew = jnp.cumsum(logf, axis=1)
    cq = jnp.transpose(cnew, (0, 2, 1))[..., None]
    ck = jnp.transpose(cnew, (0, 2, 1))[:, :, None, :]
    qf = q.astype(F32)
    s_new = jnp.einsum('bqhd,bkhd->bhqk', qf, k.astype(F32)) * scale + cq - ck
    causal = jnp.arange(S)[:, None] >= jnp.arange(S)[None, :]
    s_new = jnp.where(causal, s_new, -jnp.inf)
    m0 = jnp.max(s_new, axis=-1)
    p0 = jnp.exp(s_new - m0[..., None])
    l0 = jnp.sum(p0, axis=-1)
    acc0 = jnp.einsum('bhqk,bkhd->bhqd', p0, v.astype(F32))
    bias_pages = jnp.moveaxis(jnp.transpose(suf, (0, 2, 1)).reshape(DB, H, n_pages, PAGE_SIZE), 2, 0)

    def step(carry, xs):
        m, l, acc = carry
        phys, bias = xs
        kp = cache_k[layer, phys].astype(F32)
        vp = cache_v[layer, phys].astype(F32)
        s = jnp.einsum('bqhd,bkhd->bhqk', qf, kp) * scale + cq + bias[:, :, None, :]
        m_new = jnp.maximum(m, jnp.max(s, axis=-1))
        corr = jnp.exp(m - m_new)
        p = jnp.exp(s - m_new[..., None])
        l = l * corr + jnp.sum(p, axis=-1)
        acc = acc * corr[..., None] + jnp.einsum('bhqk,bkhd->bhqd', p, vp)
        return (m_new, l, acc), None

    (m, l, acc), _ = lax.scan(step, (m0, l0, acc0), (page_table.T, bias_pages))
    out = acc / l[..., None]
    return jnp.transpose(out, (0, 2, 1, 3))


def token_mixer(u, lp, conv_prev, h_prev, is_start, attend):
    B, T, _ = u.shape
    proj = u @ lp['w_in']
    cuts = np.cumsum([D_RNN, D_RNN, ATTN_WIDTH, ATTN_WIDTH, ATTN_WIDTH, N_HEADS, D_MODEL]).tolist()
    xb, gb, q, k, v, fg, g_lru, g_att = jnp.split(proj, cuts, axis=-1)
    xpad = jnp.concatenate([conv_prev.astype(xb.dtype), xb], axis=1)
    xc = causal_conv(xpad, lp['conv_w'], lp['conv_b'])
    hs, h_last = rg_lru(xc, lp['w_rg'], lp['b_rg'], lp['w_ig'], lp['b_ig'], lp['lru_lambda'], h_prev, is_start)
    lru_out = hs.astype(u.dtype) * jax.nn.gelu(gb)
    logf = jax.nn.log_sigmoid((fg + lp['b_f']).astype(F32))
    q = q.reshape(B, T, N_HEADS, HEAD_DIM)
    k = k.reshape(B, T, N_HEADS, HEAD_DIM)
    v = v.reshape(B, T, N_HEADS, HEAD_DIM)
    att = attend(q, k, v, logf).reshape(B, T, ATTN_WIDTH).astype(u.dtype)
    y = jax.nn.sigmoid(g_lru) * (lru_out @ lp['w_br_lru']) + jax.nn.sigmoid(g_att) * (att @ lp['w_br_attn'])
    out = y @ lp['w_out']
    return out, k, v, logf, xpad[:, -(CONV_W - 1):], h_last


def hierarchical_moe(u, w_grp, b_grp, w_rt, b_rt, w_gate, w_up, w_down):
    B, T, D = u.shape
    t = u.reshape(B * T, D)
    g_logits = (t @ w_grp + b_grp).astype(F32)
    g_prob = jax.nn.softmax(g_logits, axis=-1)
    g_idx = jnp.argmax(g_logits, axis=-1)
    g_p = jnp.take_along_axis(g_prob, g_idx[:, None], axis=1)
    e_logits = (t @ w_rt + b_rt).astype(F32).reshape(-1, N_GROUPS, EXPERTS_PER_GROUP)
    e_in = jnp.take_along_axis(e_logits, g_idx[:, None, None], axis=1)[:, 0]
    top_v, top_i = lax.top_k(e_in, TOP_K)
    w_top = jax.nn.softmax(top_v, axis=-1) * g_p
    ids = g_idx[:, None] * EXPERTS_PER_GROUP + top_i
    gates = jnp.sum(jax.nn.one_hot(ids, N_EXPERTS, dtype=F32) * w_top[..., None], axis=1)
    h = jax.nn.silu(jnp.einsum('nd,edf->nef', t, w_gate)) * jnp.einsum('nd,edf->nef', t, w_up)
    h = h * gates[..., None].astype(h.dtype)
    y = jnp.einsum('nef,efd->nd', h, w_down)
    return y.reshape(B, T, D)


def trunk_layer(x, mods, lp, conv_prev, h_prev, is_start, attend):
    shift1, scale1, gate1, shift2, scale2, gate2 = mods
    u = x * (1.0 + scale1) + shift1
    mix, k, v, logf, conv_new, h_last = token_mixer(u, lp, conv_prev, h_prev, is_start, attend)
    x = layer_norm(ALPHA * x + gate1 * mix, lp['ln1_g'], lp['ln1_b'])
    u2 = x * (1.0 + scale2) + shift2
    ffn = hierarchical_moe(u2, lp['w_grp'], lp['b_grp'], lp['w_rt'], lp['b_rt'], lp['w_gate'], lp['w_up'], lp['w_down'])
    x = layer_norm(ALPHA * x + gate2 * ffn, lp['ln2_g'], lp['ln2_b'])
    return x, (k, v, logf, conv_new, h_last)


def setup_inputs(seed: int = 0) -> dict:
    key = jax.random.key(seed)
    keys = list(jax.random.split(key, 40))

    def nk():
        return keys.pop()

    def nrm(shape, scale):
        return jax.random.normal(nk(), shape, F32) * scale

    n_pages = PAST_LEN // PAGE_SIZE
    n_used = DEC_BATCH * n_pages
    n_pool = n_used + max(1, n_used // 4)
    page_table = jax.random.permutation(nk(), n_pool)[:n_used].reshape(DEC_BATCH, n_pages).astype(jnp.int32)
    ua = jax.random.uniform(nk(), (DEPTH, D_RNN), F32, 0.9, 0.999)
    sa = ua ** (1.0 / LRU_C)
    lru_lambda = jnp.log(sa) - jnp.log1p(-sa)
    b_f = 3.0 + 2.0 * jax.random.uniform(nk(), (DEPTH, N_HEADS), F32)
    d_in = D_MODEL ** -0.5
    return {
        'x_prompt': nrm((BATCH, SEQ, D_MODEL), 1.0),
        'x_sample': nrm((DEC_BATCH, DEC_SEQ, D_MODEL), 1.0),
        'cache_k': nrm((DEPTH, n_pool, PAGE_SIZE, N_HEADS, HEAD_DIM), 1.0),
        'cache_v': nrm((DEPTH, n_pool, PAGE_SIZE, N_HEADS, HEAD_DIM), 1.0),
        'cache_logf': jax.nn.log_sigmoid(4.0 + nrm((DEPTH, n_pool, PAGE_SIZE, N_HEADS), 0.5)),
        'state_conv': nrm((DEPTH, DEC_BATCH, CONV_W - 1, D_RNN), 1.0),
        'state_h': nrm((DEPTH, DEC_BATCH, D_RNN), 0.5),
        'page_table': page_table,
        'c_prompt': nrm((BATCH, D_MODEL), 1.0),
        'c_sample': nrm((DEC_BATCH, D_MODEL), 1.0),
        'w_mod': nrm((DEPTH, D_MODEL, 6 * D_MODEL), 0.5 * d_in),
        'b_mod': nrm((DEPTH, 6 * D_MODEL), 0.02),
        'w_in': nrm((DEPTH, D_MODEL, N_IN), d_in),
        'b_f': b_f,
        'conv_w': nrm((DEPTH, CONV_W, D_RNN), CONV_W ** -0.5),
        'conv_b': nrm((DEPTH, D_RNN), 0.01),
        'w_rg': nrm((DEPTH, N_RNN_BLOCKS, RNN_BLOCK, RNN_BLOCK), RNN_BLOCK ** -0.5),
        'b_rg': nrm((DEPTH, D_RNN), 0.01),
        'w_ig': nrm((DEPTH, N_RNN_BLOCKS, RNN_BLOCK, RNN_BLOCK), RNN_BLOCK ** -0.5),
        'b_ig': nrm((DEPTH, D_RNN), 0.01),
        'lru_lambda': lru_lambda,
        'w_br_lru': nrm((DEPTH, D_RNN, D_MODEL), D_RNN ** -0.5),
        'w_br_attn': nrm((DEPTH, ATTN_WIDTH, D_MODEL), ATTN_WIDTH ** -0.5),
        'w_out': nrm((DEPTH, D_MODEL, D_MODEL), d_in * BETA),
        'ln1_g': 1.0 + nrm((DEPTH, D_MODEL), 0.02),
        'ln1_b': nrm((DEPTH, D_MODEL), 0.02),
        'w_grp': nrm((DEPTH, D_MODEL, N_GROUPS), d_in),
        'b_grp': nrm((DEPTH, N_GROUPS), 0.01),
        'w_rt': nrm((DEPTH, D_MODEL, N_EXPERTS), d_in),
        'b_rt': nrm((DEPTH, N_EXPERTS), 0.01),
        'w_gate': nrm((DEPTH, N_EXPERTS, D_MODEL, D_EXPERT), d_in),
        'w_up': nrm((DEPTH, N_EXPERTS, D_MODEL, D_EXPERT), d_in),
        'w_down': nrm((DEPTH, N_EXPERTS, D_EXPERT, D_MODEL), D_EXPERT ** -0.5 * BETA),
        'ln2_g': 1.0 + nrm((DEPTH, D_MODEL), 0.02),
        'ln2_b': nrm((DEPTH, D_MODEL), 0.02),
    }


def reference(x_prompt, x_sample, cache_k, cache_v, cache_logf, state_conv, state_h, page_table,
              c_prompt, c_sample, w_mod, b_mod, w_in, b_f, conv_w, conv_b, w_rg, b_rg, w_ig, b_ig,
              lru_lambda, w_br_lru, w_br_attn, w_out, ln1_g, ln1_b, w_grp, b_grp, w_rt, b_rt,
              w_gate, w_up, w_down, ln2_g, ln2_b):
    xp, xs = x_prompt, x_sample
    st_p, st_s = [], []
    for li in range(DEPTH):
        lp = {
            'w_in': w_in[li], 'b_f': b_f[li], 'conv_w': conv_w[li], 'conv_b': conv_b[li],
            'w_rg': w_rg[li], 'b_rg': b_rg[li], 'w_ig': w_ig[li], 'b_ig': b_ig[li],
            'lru_lambda': lru_lambda[li], 'w_br_lru': w_br_lru[li], 'w_br_attn': w_br_attn[li],
            'w_out': w_out[li], 'ln1_g': ln1_g[li], 'ln1_b': ln1_b[li],
            'w_grp': w_grp[li], 'b_grp': b_grp[li], 'w_rt': w_rt[li], 'b_rt': b_rt[li],
            'w_gate': w_gate[li], 'w_up': w_up[li], 'w_down': w_down[li],
            'ln2_g': ln2_g[li], 'ln2_b': ln2_b[li],
        }
        mods_p = adaln(c_prompt, w_mod[li], b_mod[li])
        mods_s = adaln(c_sample, w_mod[li], b_mod[li])
        conv0 = jnp.zeros((xp.shape[0], CONV_W - 1, D_RNN), xp.dtype)
        h0 = jnp.zeros((xp.shape[0], D_RNN), F32)
        xp, sp = trunk_layer(xp, mods_p, lp, conv0, h0, True, forgetting_attention_prompt)
        attend_s = functools.partial(forgetting_attention_sample, cache_k=cache_k, cache_v=cache_v,
                                     cache_logf=cache_logf, page_table=page_table, layer=li)
        xs, ss = trunk_layer(xs, mods_s, lp, state_conv[li], state_h[li], False, attend_s)
        st_p.append(sp)
        st_s.append(ss)
    k_prompt = jnp.stack([s[0] for s in st_p])
    v_prompt = jnp.stack([s[1] for s in st_p])
    logf_prompt = jnp.stack([s[2] for s in st_p])
    conv_prompt = jnp.stack([s[3] for s in st_p])
    h_prompt = jnp.stack([s[4] for s in st_p])
    k_sample = jnp.stack([s[0] for s in st_s])
    v_sample = jnp.stack([s[1] for s in st_s])
    logf_sample = jnp.stack([s[2] for s in st_s])
    conv_sample = jnp.stack([s[3] for s in st_s])
    h_sample = jnp.stack([s[4] for s in st_s])
    return (xp, xs, k_prompt, v_prompt, logf_prompt, conv_prompt, h_prompt,
            k_sample, v_sample, logf_sample, conv_sample, h_sample)
```

```python
import functools
import math

import jax
import jax.numpy as jnp
from jax import lax
from jax.experimental import pallas as pl
from jax.experimental.pallas import tpu as pltpu

F32 = jnp.float32
BF16 = jnp.bfloat16

N_HEADS = 8
N_RNN_BLOCKS = 8
CONV_W = 4
LRU_C = 8.0
N_GROUPS = 4
EXPERTS_PER_GROUP = 8
N_EXPERTS = N_GROUPS * EXPERTS_PER_GROUP
LN_EPS = 1e-5
LANES = 128
SUBLANES = 8
MXU_N = 256
VMEM_LIMIT = 56 * 1024 * 1024

NEG_INF = float("-inf")


def _cparams(sem):
    return pltpu.CompilerParams(dimension_semantics=sem, vmem_limit_bytes=VMEM_LIMIT)


def _split3(x):
    hi = x.astype(BF16)
    r1 = x - hi.astype(F32)
    mid = r1.astype(BF16)
    lo = (r1 - mid.astype(F32)).astype(BF16)
    return hi, mid, lo


def _dot(a, b):
    return jnp.dot(a, b, preferred_element_type=F32)


def _layer_norm(z, g, b):
    mu = jnp.mean(z, axis=-1, keepdims=True)
    zc = z - mu
    var = jnp.mean(zc * zc, axis=-1, keepdims=True)
    return zc * lax.rsqrt(var + LN_EPS) * g + b


def _log_sigmoid(x):
    return jnp.minimum(x, 0.0) - jnp.log1p(jnp.exp(-jnp.abs(x)))


def _softplus(x):
    return jnp.maximum(x, 0.0) + jnp.log1p(jnp.exp(-jnp.abs(x)))


def _gelu_tanh(x):
    c = math.sqrt(2.0 / math.pi)
    return 0.5 * x * (1.0 + jnp.tanh(c * (x + 0.044715 * (x * x * x))))


def _mod_spec(per_row, tm, d, chunk, tiles_per_batch):
    if per_row:
        return pl.BlockSpec((None, tm, d), lambda i: (0, i, chunk))
    return pl.BlockSpec((None, 1, d), lambda i: (i // tiles_per_batch, 0, chunk))


def _mods_kernel(c_ref, w_ref, b_ref, o_ref):
    c = c_ref[...]
    s = (c * jax.nn.sigmoid(c)).astype(BF16)
    o_ref[...] = _dot(s, w_ref[...].astype(BF16)) + b_ref[...]


def _mods_call(c_all, w_mod, b_mod):
    rc, d = c_all.shape
    n = w_mod.shape[1]
    tn = 1536
    return pl.pallas_call(
        _mods_kernel,
        grid=(n // tn,),
        in_specs=[pl.BlockSpec((rc, d), lambda j: (0, 0)),
                  pl.BlockSpec((d, tn), lambda j: (0, j)),
                  pl.BlockSpec((1, tn), lambda j: (0, j))],
        out_specs=pl.BlockSpec((rc, tn), lambda j: (0, j)),
        out_shape=jax.ShapeDtypeStruct((rc, n), F32),
        compiler_params=_cparams(("arbitrary",)),
        name="adaln_mods",
    )(c_all, w_mod, b_mod.reshape(1, n))


def _inproj_kernel(x_ref, shift_ref, scale_ref, w_ref, bf_ref, tri_ref,
                   xb_ref, gb_ref, q_ref, k_ref, v_ref, kb_ref, vb_ref, gl_ref, ga_ref, logf_ref, cum_ref,
                   carry_ref, *, tiles_per_batch, q_scale):
    i = pl.program_id(0)
    d = x_ref.shape[1]
    tm = x_ref.shape[0]
    u = (x_ref[...] * (1.0 + scale_ref[...]) + shift_ref[...]).astype(BF16)

    def mm(c):
        return _dot(u, w_ref[:, c * d:(c + 1) * d])

    xb_ref[...] = mm(0)
    gb_ref[...] = mm(1)
    q_ref[...] = (mm(2) * q_scale).astype(BF16)
    k = mm(3)
    k_ref[...] = k
    kb_ref[...] = k.astype(BF16)
    v = mm(4)
    v_ref[...] = v
    vb_ref[...] = v.astype(BF16)
    gl_ref[...] = mm(5)
    ga_ref[...] = mm(6)
    fg = _dot(u, w_ref[:, 7 * d:7 * d + LANES]) + bf_ref[...]
    logf = _log_sigmoid(fg)
    logf_ref[...] = logf[:, :N_HEADS]

    @pl.when(i % tiles_per_batch == 0)
    def _():
        carry_ref[...] = jnp.zeros_like(carry_ref)

    tri = tri_ref[...]
    hi, mid, lo = _split3(logf)
    cum = _dot(tri, hi) + _dot(tri, mid) + _dot(tri, lo) + carry_ref[...]
    cum_ref[...] = cum[:, :N_HEADS]
    carry_ref[...] = cum[tm - 1:tm, :]


def _inproj_call(x, mods, w_cat, bf_pad, *, per_row, rows_per_batch, q_scale):
    r, d = x.shape
    tm = min(256, r)
    tiles_per_batch = max(rows_per_batch // tm, 1)
    tri = (jnp.arange(tm)[:, None] >= jnp.arange(tm)[None, :]).astype(BF16)
    row_f32 = pl.BlockSpec((tm, d), lambda i: (i, 0))
    small = pl.BlockSpec((tm, N_HEADS), lambda i: (i, 0))
    sd = lambda dt: jax.ShapeDtypeStruct((r, d), dt)
    sh = jax.ShapeDtypeStruct((r, N_HEADS), F32)
    return pl.pallas_call(
        functools.partial(_inproj_kernel, tiles_per_batch=tiles_per_batch, q_scale=q_scale),
        grid=(r // tm,),
        in_specs=[row_f32,
                  _mod_spec(per_row, tm, d, 0, tiles_per_batch),
                  _mod_spec(per_row, tm, d, 1, tiles_per_batch),
                  pl.BlockSpec(w_cat.shape, lambda i: (0, 0), pipeline_mode=pl.Buffered(1)),
                  pl.BlockSpec((1, LANES), lambda i: (0, 0)),
                  pl.BlockSpec((tm, tm), lambda i: (0, 0))],
        out_specs=[row_f32, row_f32, row_f32, row_f32, row_f32, row_f32, row_f32, row_f32, row_f32, small, small],
        out_shape=[sd(F32), sd(F32), sd(BF16), sd(F32), sd(F32), sd(BF16), sd(BF16), sd(F32), sd(F32), sh, sh],
        scratch_shapes=[pltpu.VMEM((1, LANES), F32)],
        compiler_params=_cparams(("arbitrary",)),
        name="in_projection",
    )(x, mods, mods, w_cat, bf_pad, tri)


def _lru_gates(xc, wrg_ref, brg, wig_ref, big, lam):
    blk = xc.shape[1] // N_RNN_BLOCKS
    rs, is_ = [], []
    for n in range(N_RNN_BLOCKS):
        xn = xc[:, n * blk:(n + 1) * blk].astype(BF16)
        rs.append(_dot(xn, wrg_ref[n]))
        is_.append(_dot(xn, wig_ref[n]))
    r = jax.nn.sigmoid(jnp.concatenate(rs, axis=1) + brg)
    ig = jax.nn.sigmoid(jnp.concatenate(is_, axis=1) + big)
    log_a = (-LRU_C) * r * _softplus(-lam)
    a = jnp.exp(log_a)
    mult = jnp.sqrt(1.0 - jnp.exp(2.0 * log_a))
    return a, mult, ig


def _lru_prompt_kernel(xb_ref, gb_ref, cw_ref, cb_ref, wrg_ref, brg_ref, wig_ref, big_ref, lam_ref,
                       out_ref, hlast_ref, xp_ref, a_ref, b_ref, hs_ref, h_ref):
    t = pl.program_id(1)
    nt = pl.num_programs(1)
    tm = xb_ref.shape[0]
    halo = SUBLANES

    @pl.when(t == 0)
    def _():
        xp_ref[0:halo, :] = jnp.zeros((halo, xp_ref.shape[1]), F32)
        h_ref[...] = jnp.zeros_like(h_ref)

    xp_ref[halo:halo + tm, :] = xb_ref[...]
    xc = cb_ref[...]
    for tap in range(CONV_W):
        off = halo - (CONV_W - 1) + tap
        xc = xc + xp_ref[off:off + tm, :] * cw_ref[tap:tap + 1, :]
    a, mult, ig = _lru_gates(xc, wrg_ref, brg_ref[...], wig_ref, big_ref[...], lam_ref[...])
    row = lax.broadcasted_iota(jnp.int32, (tm, 1), 0)
    mult = jnp.where((row == 0) & (t == 0), 1.0, mult)
    a_ref[...] = a
    b_ref[...] = mult * ig * xc

    def step(s, h):
        h = a_ref[pl.ds(s, 1), :] * h + b_ref[pl.ds(s, 1), :]
        hs_ref[pl.ds(s, 1), :] = h
        return h

    h = lax.fori_loop(0, tm, step, h_ref[...], unroll=8)
    h_ref[...] = h
    xp_ref[halo - (CONV_W - 1):halo, :] = xp_ref[halo + tm - (CONV_W - 1):halo + tm, :]
    out_ref[...] = (hs_ref[...] * _gelu_tanh(gb_ref[...])).astype(BF16)

    @pl.when(t == nt - 1)
    def _():
        hlast_ref[...] = h


def _lru_prompt_call(xb, gb, b, t_len, lw):
    r, d = xb.shape
    tm = min(256, t_len)
    nt = t_len // tm
    row = pl.BlockSpec((tm, d), lambda bi, ti: (bi * nt + ti, 0))
    vec = pl.BlockSpec((1, d), lambda bi, ti: (0, 0))
    blk = d // N_RNN_BLOCKS
    wspec = pl.BlockSpec((N_RNN_BLOCKS, blk, blk), lambda bi, ti: (0, 0, 0))
    return pl.pallas_call(
        _lru_prompt_kernel,
        grid=(b, nt),
        in_specs=[row, row, pl.BlockSpec((CONV_W, d), lambda bi, ti: (0, 0)), vec, wspec, vec, wspec, vec, vec],
        out_specs=[row, pl.BlockSpec((None, 1, d), lambda bi, ti: (bi, 0, 0))],
        out_shape=[jax.ShapeDtypeStruct((r, d), BF16), jax.ShapeDtypeStruct((b, 1, d), F32)],
        scratch_shapes=[pltpu.VMEM((tm + SUBLANES, d), F32), pltpu.VMEM((tm, d), F32), pltpu.VMEM((tm, d), F32),
                        pltpu.VMEM((tm, d), F32), pltpu.VMEM((1, d), F32)],
        compiler_params=_cparams(("arbitrary", "arbitrary")),
        name="rglru_prompt",
    )(xb, gb, lw["conv_w"], lw["conv_b"], lw["w_rg"], lw["b_rg"], lw["w_ig"], lw["b_ig"], lw["lam"])


def _lru_sample_kernel(xb_ref, gb_ref, c0_ref, c1_ref, c2_ref, hprev_ref, cw_ref, cb_ref,
                       wrg_ref, brg_ref, wig_ref, big_ref, lam_ref, out_ref, hnew_ref):
    xb = xb_ref[...]
    xc = (cb_ref[...] + c0_ref[...] * cw_ref[0:1, :] + c1_ref[...] * cw_ref[1:2, :]
          + c2_ref[...] * cw_ref[2:3, :] + xb * cw_ref[3:4, :])
    a, mult, ig = _lru_gates(xc, wrg_ref, brg_ref[...], wig_ref, big_ref[...], lam_ref[...])
    h = a * hprev_ref[...] + mult * ig * xc
    hnew_ref[...] = h
    out_ref[...] = (h * _gelu_tanh(gb_ref[...])).astype(BF16)


def _lru_sample_call(xb, gb, conv_taps, h_prev, lw):
    r, d = xb.shape
    return pl.pallas_call(
        _lru_sample_kernel,
        out_shape=[jax.ShapeDtypeStruct((r, d), BF16), jax.ShapeDtypeStruct((r, d), F32)],
        compiler_params=pltpu.CompilerParams(vmem_limit_bytes=VMEM_LIMIT),
        name="rglru_sample",
    )(xb, gb, conv_taps[0], conv_taps[1], conv_taps[2], h_prev, lw["conv_w"], lw["conv_b"],
      lw["w_rg"], lw["b_rg"], lw["w_ig"], lw["b_ig"], lw["lam"])


def _attn_prompt_kernel(qi_ref, ki_ref, q_ref, k_ref, v_ref, fq_ref, fk_ref, o_ref, m_ref, l_ref, acc_ref):
    p_id = pl.program_id(2)
    qi = qi_ref[p_id]
    ki = ki_ref[p_id]
    tq, tk = q_ref.shape[0], k_ref.shape[0]

    @pl.when(ki == 0)
    def _():
        m_ref[...] = jnp.full_like(m_ref, NEG_INF)
        l_ref[...] = jnp.zeros_like(l_ref)
        acc_ref[...] = jnp.zeros_like(acc_ref)

    def update(masked):
        s = lax.dot_general(q_ref[...], k_ref[...], (((1,), (1,)), ((), ())), preferred_element_type=F32)
        s = s + fq_ref[...] - fk_ref[...]
        if masked:
            row = lax.broadcasted_iota(jnp.int32, (tq, tk), 0)
            col = lax.broadcasted_iota(jnp.int32, (tq, tk), 1)
            s = jnp.where(row >= col, s, NEG_INF)
        m_prev = m_ref[...]
        m_new = jnp.maximum(m_prev, jnp.max(s, axis=-1, keepdims=True))
        corr = jnp.exp(m_prev - m_new)
        p = jnp.exp(s - m_new)
        l_ref[...] = corr * l_ref[...] + jnp.sum(p, axis=-1, keepdims=True)
        acc_ref[...] = corr * acc_ref[...] + _dot(p.astype(BF16), v_ref[...])
        m_ref[...] = m_new

    @pl.when(ki < qi)
    def _():
        update(False)

    @pl.when(ki == qi)
    def _():
        update(True)
        o_ref[...] = (acc_ref[...] / l_ref[...]).astype(o_ref.dtype)


def _attn_prompt_call(q, kb, vb, cum, b, t_len):
    r, d = q.shape
    hd = d // N_HEADS
    tq = min(512, t_len)
    nq = t_len // tq
    pairs = [(a, c) for a in range(nq) for c in range(a + 1)]
    qi_tab = jnp.asarray([p[0] for p in pairs], jnp.int32)
    ki_tab = jnp.asarray([p[1] for p in pairs], jnp.int32)
    cum_hk = jnp.transpose(cum.reshape(b, t_len, N_HEADS), (0, 2, 1))
    fq = cum_hk.reshape(b, N_HEADS, t_len, 1)
    fk = cum_hk.reshape(b, N_HEADS, 1, t_len)
    grid_spec = pltpu.PrefetchScalarGridSpec(
        num_scalar_prefetch=2,
        grid=(b, N_HEADS, len(pairs)),
        in_specs=[pl.BlockSpec((tq, hd), lambda bi, h, p, qt, kt: (bi * nq + qt[p], h)),
                  pl.BlockSpec((tq, hd), lambda bi, h, p, qt, kt: (bi * nq + kt[p], h)),
                  pl.BlockSpec((tq, hd), lambda bi, h, p, qt, kt: (bi * nq + kt[p], h)),
                  pl.BlockSpec((None, None, tq, 1), lambda bi, h, p, qt, kt: (bi, h, qt[p], 0)),
                  pl.BlockSpec((None, None, 1, tq), lambda bi, h, p, qt, kt: (bi, h, 0, kt[p]))],
        out_specs=pl.BlockSpec((tq, hd), lambda bi, h, p, qt, kt: (bi * nq + qt[p], h)),
        scratch_shapes=[pltpu.VMEM((tq, 1), F32), pltpu.VMEM((tq, 1), F32), pltpu.VMEM((tq, hd), F32)],
    )
    return pl.pallas_call(
        _attn_prompt_kernel,
        grid_spec=grid_spec,
        out_shape=jax.ShapeDtypeStruct((r, d), BF16),
        compiler_params=_cparams(("arbitrary", "arbitrary", "arbitrary")),
        name="fox_attention_prompt",
    )(qi_tab, ki_tab, q, kb, vb, fq, fk)


def _logf_pages_kernel(l_ref, u_ref, ones_ref, pe_ref, po_ref, c_ref, tot_ref):
    parts = _split3(l_ref[...])
    insuf = sum(_dot(p, u_ref[...]) for p in parts)
    tot_ref[...] = sum(_dot(p, ones_ref[...]) for p in parts)
    hi = insuf.astype(BF16)
    mid = (insuf - hi.astype(F32)).astype(BF16)
    c_ref[...] = _dot(hi, pe_ref[...]) + _dot(mid, po_ref[...])


def _logf_pages_call(cache_logf):
    n_pool, page, h = cache_logf.shape
    rows = n_pool * h
    lft = jnp.swapaxes(cache_logf, 1, 2).reshape(rows, page)
    kk = jnp.arange(page)
    later = (kk[:, None] > kk[None, :]).astype(BF16)
    ones = jnp.ones((page, page), BF16)
    pe = (2 * kk[:, None] == jnp.arange(2 * page)[None, :]).astype(BF16)
    po = (2 * kk[:, None] + 1 == jnp.arange(2 * page)[None, :]).astype(BF16)
    tr = 2048
    while rows % tr:
        tr //= 2
    const = lambda shape: pl.BlockSpec(shape, lambda i: (0, 0))
    c, tot = pl.pallas_call(
        _logf_pages_kernel,
        grid=(rows // tr,),
        in_specs=[pl.BlockSpec((tr, page), lambda i: (i, 0)), const((page, page)), const((page, page)),
                  const((page, 2 * page)), const((page, 2 * page))],
        out_specs=[pl.BlockSpec((tr, 2 * page), lambda i: (i, 0)), pl.BlockSpec((tr, page), lambda i: (i, 0))],
        out_shape=[jax.ShapeDtypeStruct((rows, 2 * page), F32), jax.ShapeDtypeStruct((rows, page), F32)],
        compiler_params=_cparams(("arbitrary",)),
        name="logf_page_sums",
    )(lft, later, ones, pe, po)
    return c.reshape(n_pool, h, 2 * page), tot.reshape(n_pool, h, page)


def _attn_sample_kernel(pt_ref, q_ref, knew_ref, vnew_ref, cq_ref, ones_ref, *refs, pages_per_step):
    pp = pages_per_step
    k_refs, v_refs = refs[0:pp], refs[pp:2 * pp]
    c_refs, tot_refs = refs[2 * pp:3 * pp], refs[3 * pp:4 * pp]
    o_ref = refs[4 * pp]
    m_ref, l_ref, acc_ref, psuf_ref = refs[4 * pp + 1:]
    j = pl.program_id(1)
    nj = pl.num_programs(1)
    page, h, hd = k_refs[0].shape
    q = q_ref[...]

    def lane_sums(lhs):
        return _dot(lhs.astype(BF16), ones_ref[...])

    @pl.when(j == 0)
    def _():
        prod = jnp.concatenate([knew_ref[...] * q, jnp.zeros((h, hd), F32)], axis=1)
        prod = jnp.concatenate([prod, jnp.zeros_like(prod)], axis=0)
        m_ref[...] = lane_sums(prod)[:h]
        l_ref[...] = jnp.ones_like(l_ref)
        acc_ref[...] = vnew_ref[...]
        psuf_ref[...] = cq_ref[...]

    half = page // 2
    key = lax.broadcasted_iota(jnp.int32, (half, h, hd), 0)
    lane = lax.broadcasted_iota(jnp.int32, (half, h, hd), 2)
    own = (lane >> 1) == key
    for i in range(pp):
        prod = k_refs[i][...] * q[None]
        c = c_refs[i][...]
        bias = jnp.concatenate([jnp.where(own, c[None, :, :hd], 0.0), jnp.where(own, c[None, :, hd:], 0.0)], axis=0)
        lhs = jnp.concatenate([prod, bias], axis=2).reshape(page * h, 2 * hd)
        s = lane_sums(lhs).reshape(page, h, hd) + psuf_ref[...][None]
        m_old = m_ref[...]
        m_new = jnp.maximum(m_old, jnp.max(s, axis=0))
        corr = jnp.exp(m_old - m_new)
        p = jnp.exp(s - m_new[None])
        l_ref[...] = l_ref[...] * corr + jnp.sum(p, axis=0)
        acc_ref[...] = acc_ref[...] * corr + jnp.sum(p * v_refs[i][...], axis=0)
        m_ref[...] = m_new
        psuf_ref[...] = psuf_ref[...] + tot_refs[i][...]

    @pl.when(j == nj - 1)
    def _():
        o_ref[...] = acc_ref[...] / l_ref[...]


def _attn_sample_call(q, k_new, v_new, logf_new, cache_k, cache_v, c_pages, tot_pages, page_table):
    db, d = q.shape
    n_pool, page, h, hd = cache_k.shape
    assert page == hd, "the two bias terms of a page's keys are packed into 2*HEAD_DIM lanes"
    n_pages = page_table.shape[1]
    pp = 8
    while n_pages % pp:
        pp //= 2
    n_steps = n_pages // pp
    cq = jnp.broadcast_to(logf_new[:, :, None], (db, h, hd))
    ones = jnp.ones((2 * hd, hd), BF16)

    def page_of(i):
        return lambda bi, j, pt: (pt[bi, n_pages - 1 - (j * pp + i)], 0, 0)

    def page4_of(i):
        return lambda bi, j, pt: (pt[bi, n_pages - 1 - (j * pp + i)], 0, 0, 0)

    tile = pl.BlockSpec((None, h, hd), lambda bi, j, pt: (bi, 0, 0))
    in_specs = [tile, tile, tile, tile, pl.BlockSpec((2 * hd, hd), lambda bi, j, pt: (0, 0))]
    in_specs += [pl.BlockSpec((None, page, h, hd), page4_of(i)) for i in range(pp)]
    in_specs += [pl.BlockSpec((None, page, h, hd), page4_of(i)) for i in range(pp)]
    in_specs += [pl.BlockSpec((None, h, 2 * hd), page_of(i)) for i in range(pp)]
    in_specs += [pl.BlockSpec((None, h, hd), page_of(i)) for i in range(pp)]
    grid_spec = pltpu.PrefetchScalarGridSpec(
        num_scalar_prefetch=1,
        grid=(db, n_steps),
        in_specs=in_specs,
        out_specs=tile,
        scratch_shapes=[pltpu.VMEM((h, hd), F32)] * 4,
    )
    out = pl.pallas_call(
        functools.partial(_attn_sample_kernel, pages_per_step=pp),
        grid_spec=grid_spec,
        out_shape=jax.ShapeDtypeStruct((db, h, hd), F32),
        compiler_params=_cparams(("arbitrary", "arbitrary")),
        name="fox_attention_sample",
    )(page_table, q.astype(F32).reshape(db, h, hd), k_new.reshape(db, h, hd), v_new.reshape(db, h, hd), cq, ones,
      *([cache_k] * pp), *([cache_v] * pp), *([c_pages] * pp), *([tot_pages] * pp))
    return out.reshape(db, d)


def _router_gates(logits):
    lane = lax.broadcasted_iota(jnp.int32, logits.shape, 1)
    lane_f = lane.astype(F32)
    big = float(4 * LANES)
    is_grp = (lane >= N_EXPERTS) & (lane < N_EXPERTS + N_GROUPS)
    gl = jnp.where(is_grp, logits, NEG_INF)
    gmax = jnp.max(gl, axis=-1, keepdims=True)
    gidx = jnp.min(jnp.where(gl == gmax, lane_f - N_EXPERTS, big), axis=-1, keepdims=True)
    g_p = 1.0 / jnp.sum(jnp.exp(gl - gmax), axis=-1, keepdims=True)
    in_grp = (lane < N_EXPERTS) & ((lane // EXPERTS_PER_GROUP).astype(F32) == gidx)
    e1 = jnp.where(in_grp, logits, NEG_INF)
    m1 = jnp.max(e1, axis=-1, keepdims=True)
    i1 = jnp.min(jnp.where(e1 == m1, lane_f, big), axis=-1, keepdims=True)
    e2 = jnp.where(lane_f == i1, NEG_INF, e1)
    m2 = jnp.max(e2, axis=-1, keepdims=True)
    i2 = jnp.min(jnp.where(e2 == m2, lane_f, big), axis=-1, keepdims=True)
    dlt = jnp.exp(m2 - m1)
    w1 = g_p / (1.0 + dlt)
    w2 = g_p * dlt / (1.0 + dlt)
    return jnp.where(lane_f == i1, w1, 0.0) + jnp.where(lane_f == i2, w2, 0.0)


def _merge_kernel(x_ref, lru_ref, att_ref, gl_ref, ga_ref, gate1_ref, shift2_ref, scale2_ref,
                  wl_ref, wa_ref, wo_ref, g_ref, b_ref, wr_ref, br_ref,
                  x1_ref, u2_ref, gates_ref, *, alpha):
    y = (jax.nn.sigmoid(gl_ref[...]) * _dot(lru_ref[...], wl_ref[...])
         + jax.nn.sigmoid(ga_ref[...]) * _dot(att_ref[...], wa_ref[...]))
    mix = _dot(y.astype(BF16), wo_ref[...])
    x1 = _layer_norm(alpha * x_ref[...] + gate1_ref[...] * mix, g_ref[...], b_ref[...])
    x1_ref[...] = x1
    u2 = x1 * (1.0 + scale2_ref[...]) + shift2_ref[...]
    u2_ref[...] = u2.astype(BF16)
    logits = jnp.dot(u2, wr_ref[...], preferred_element_type=F32, precision=lax.Precision.HIGHEST) + br_ref[...]
    gates_ref[...] = _router_gates(logits)


def _merge_call(x, lru, att, g_lru, g_att, mods, lw, *, per_row, rows_per_batch, alpha):
    r, d = x.shape
    tm = min(256, r)
    tpb = max(rows_per_batch // tm, 1)
    row = pl.BlockSpec((tm, d), lambda i: (i, 0))
    wsq = pl.BlockSpec((d, d), lambda i: (0, 0))
    vec = pl.BlockSpec((1, d), lambda i: (0, 0))
    return pl.pallas_call(
        functools.partial(_merge_kernel, alpha=alpha),
        grid=(r // tm,),
        in_specs=[row, row, row, row, row,
                  _mod_spec(per_row, tm, d, 2, tpb), _mod_spec(per_row, tm, d, 3, tpb), _mod_spec(per_row, tm, d, 4, tpb),
                  wsq, wsq, wsq, vec, vec,
                  pl.BlockSpec((d, LANES), lambda i: (0, 0)), pl.BlockSpec((1, LANES), lambda i: (0, 0))],
        out_specs=[row, row, pl.BlockSpec((tm, LANES), lambda i: (i, 0))],
        out_shape=[jax.ShapeDtypeStruct((r, d), F32), jax.ShapeDtypeStruct((r, d), BF16),
                   jax.ShapeDtypeStruct((r, LANES), F32)],
        compiler_params=_cparams(("arbitrary",)),
        name="merge_ln_router",
    )(x, lru, att, g_lru, g_att, mods, mods, mods, lw["w_br_lru"], lw["w_br_attn"], lw["w_out"],
      lw["ln1_g"], lw["ln1_b"], lw["w_router"], lw["b_router"])


def _moe_kernel(u_ref, gates_ref, wg_ref, wu_ref, wd_ref, x1_ref, gate2_ref, g_ref, b_ref, y_ref, acc_ref, *, alpha):
    e = pl.program_id(1)
    ne = pl.num_programs(1)

    @pl.when(e == 0)
    def _():
        acc_ref[...] = jnp.zeros_like(acc_ref)

    u = u_ref[...]
    lane = lax.broadcasted_iota(jnp.int32, gates_ref.shape, 1)
    ge = jnp.sum(jnp.where(lane == e, gates_ref[...], 0.0), axis=-1, keepdims=True)
    hg = _dot(u, wg_ref[...])
    hid = hg * jax.nn.sigmoid(hg) * _dot(u, wu_ref[...]) * ge
    acc_ref[...] += _dot(hid.astype(BF16), wd_ref[...])

    @pl.when(e == ne - 1)
    def _():
        y_ref[...] = _layer_norm(alpha * x1_ref[...] + gate2_ref[...] * acc_ref[...], g_ref[...], b_ref[...])


def _moe_call(u2, gates, x1, mods, lw, *, per_row, rows_per_batch, alpha):
    r, d = x1.shape
    ne, _, f = lw["w_gate"].shape
    tm = min(512, r)
    tpb = max(rows_per_batch // tm, 1)
    row = lambda w: pl.BlockSpec((tm, w), lambda i, e: (i, 0))
    vec = pl.BlockSpec((1, d), lambda i, e: (0, 0))
    if per_row:
        gate2 = pl.BlockSpec((None, tm, d), lambda i, e: (0, i, 5))
    else:
        gate2 = pl.BlockSpec((None, 1, d), lambda i, e: (i // tpb, 0, 5))
    return pl.pallas_call(
        functools.partial(_moe_kernel, alpha=alpha),
        grid=(r // tm, ne),
        in_specs=[row(d), row(LANES),
                  pl.BlockSpec((None, d, f), lambda i, e: (e, 0, 0)),
                  pl.BlockSpec((None, d, f), lambda i, e: (e, 0, 0)),
                  pl.BlockSpec((None, f, d), lambda i, e: (e, 0, 0)),
                  row(d), gate2, vec, vec],
        out_specs=row(d),
        out_shape=jax.ShapeDtypeStruct((r, d), F32),
        scratch_shapes=[pltpu.VMEM((tm, d), F32)],
        compiler_params=_cparams(("arbitrary", "arbitrary")),
        name="moe_ffn_ln",
    )(u2, gates, lw["w_gate"], lw["w_up"], lw["w_down"], x1, mods, lw["ln2_g"], lw["ln2_b"])


def _prep_layer_weights(li, w_in, b_f, conv_w, conv_b, w_rg, b_rg, w_ig, b_ig, lru_lambda, w_br_lru, w_br_attn,
                        w_out, ln1_g, ln1_b, w_grp, b_grp, w_rt, b_rt, w_gate, w_up, w_down, ln2_g, ln2_b):
    d = w_in.shape[1]
    h = b_f.shape[1]
    wi = w_in[li]
    c_main = 5 * d
    w_f = jnp.pad(wi[:, c_main:c_main + h], ((0, 0), (0, LANES - h)))
    w_cat = jnp.concatenate([wi[:, :c_main], wi[:, c_main + h:], w_f], axis=1).astype(BF16)
    w_router = jnp.pad(jnp.concatenate([w_rt[li], w_grp[li]], axis=1), ((0, 0), (0, LANES - N_EXPERTS - N_GROUPS)))
    b_router = jnp.pad(jnp.concatenate([b_rt[li], b_grp[li]]), (0, LANES - N_EXPERTS - N_GROUPS)).reshape(1, LANES)
    row = lambda v: v.reshape(1, -1)
    return {
        "w_cat": w_cat, "bf_pad": jnp.pad(b_f[li], (0, LANES - h)).reshape(1, LANES),
        "conv_w": conv_w[li], "conv_b": row(conv_b[li]),
        "w_rg": w_rg[li].astype(BF16), "b_rg": row(b_rg[li]), "w_ig": w_ig[li].astype(BF16), "b_ig": row(b_ig[li]),
        "lam": row(lru_lambda[li]),
        "w_br_lru": w_br_lru[li].astype(BF16), "w_br_attn": w_br_attn[li].astype(BF16), "w_out": w_out[li].astype(BF16),
        "ln1_g": row(ln1_g[li]), "ln1_b": row(ln1_b[li]), "ln2_g": row(ln2_g[li]), "ln2_b": row(ln2_b[li]),
        "w_router": w_router, "b_router": b_router,
        "w_gate": w_gate[li].astype(BF16), "w_up": w_up[li].astype(BF16), "w_down": w_down[li].astype(BF16),
    }


def kernel(x_prompt, x_sample, cache_k, cache_v, cache_logf, state_conv, state_h, page_table, c_prompt, c_sample, w_mod, b_mod, w_in, b_f, conv_w, conv_b, w_rg, b_rg, w_ig, b_ig, lru_lambda, w_br_lru, w_br_attn, w_out, ln1_g, ln1_b, w_grp, b_grp, w_rt, b_rt, w_gate, w_up, w_down, ln2_g, ln2_b):
    b, t_len, d = x_prompt.shape
    db, s_len, _ = x_sample.shape
    depth = w_mod.shape[0]
    h = N_HEADS
    hd = d // h
    assert s_len == 1, "the sample group decodes one token per sequence"
    assert t_len >= CONV_W - 1
    alpha = (2.0 * depth) ** 0.25
    q_scale = hd ** -0.5

    xp = x_prompt.reshape(b * t_len, d)
    xs = x_sample.reshape(db, d)
    rc = -(-(b + db) // SUBLANES) * SUBLANES
    c_all = jnp.concatenate([c_prompt, c_sample, jnp.zeros((rc - b - db, d), F32)], axis=0)
    st_p, st_s = [], []
    for li in range(depth):
        lw = _prep_layer_weights(li, w_in, b_f, conv_w, conv_b, w_rg, b_rg, w_ig, b_ig, lru_lambda, w_br_lru,
                                 w_br_attn, w_out, ln1_g, ln1_b, w_grp, b_grp, w_rt, b_rt, w_gate, w_up, w_down,
                                 ln2_g, ln2_b)
        mods = _mods_call(c_all, w_mod[li], b_mod[li])
        mods_p = mods[:b].reshape(b, 1, 6 * d)
        mods_s = mods[b:b + db].reshape(1, db, 6 * d)

        xb, gb, q, k, v, kb, vb, g_lru, g_att, logf, cum = _inproj_call(
            xp, mods_p, lw["w_cat"], lw["bf_pad"], per_row=False, rows_per_batch=t_len, q_scale=q_scale)
        lru, h_last = _lru_prompt_call(xb, gb, b, t_len, lw)
        att = _attn_prompt_call(q, kb, vb, cum, b, t_len)
        x1, u2, gates = _merge_call(xp, lru, att, g_lru, g_att, mods_p, lw, per_row=False, rows_per_batch=t_len, alpha=alpha)
        xp = _moe_call(u2, gates, x1, mods_p, lw, per_row=False, rows_per_batch=t_len, alpha=alpha)
        st_p.append((k.reshape(b, t_len, h, hd), v.reshape(b, t_len, h, hd), logf.reshape(b, t_len, h),
                     xb.reshape(b, t_len, d)[:, t_len - (CONV_W - 1):], h_last.reshape(b, d)))

        xb_s, gb_s, q_s, k_s, v_s, _, _, gl_s, ga_s, logf_s, _ = _inproj_call(
            xs, mods_s, lw["w_cat"], lw["bf_pad"], per_row=True, rows_per_batch=1, q_scale=q_scale)
        taps = [state_conv[li][:, tap, :] for tap in range(CONV_W - 1)]
        lru_s, h_new = _lru_sample_call(xb_s, gb_s, taps, state_h[li], lw)
        b3_pages, tot_pages = _logf_pages_call(cache_logf[li])
        att_s = _attn_sample_call(q_s, k_s, v_s, logf_s, cache_k[li], cache_v[li], b3_pages, tot_pages, page_table)
        x1_s, u2_s, gates_s = _merge_call(xs, lru_s, att_s.astype(BF16), gl_s, ga_s, mods_s, lw,
                                          per_row=True, rows_per_batch=1, alpha=alpha)
        xs = _moe_call(u2_s, gates_s, x1_s, mods_s, lw, per_row=True, rows_per_batch=1, alpha=alpha)
        conv_new = jnp.concatenate([state_conv[li][:, 1:], xb_s[:, None, :]], axis=1)
        st_s.append((k_s.reshape(db, 1, h, hd), v_s.reshape(db, 1, h, hd), logf_s.reshape(db, 1, h), conv_new, h_new))

    stack = lambda sts, i: jnp.stack([s[i] for s in sts])
    return (xp.reshape(b, t_len, d), xs.reshape(db, 1, d),
            stack(st_p, 0), stack(st_p, 1), stack(st_p, 2), stack(st_p, 3), stack(st_p, 4),
            stack(st_s, 0), stack(st_s, 1), stack(st_s, 2), stack(st_s, 3), stack(st_s, 4))
```

```python
import functools
import math

import jax
import jax.numpy as jnp
from jax import lax
from jax.experimental import pallas as pl
from jax.experimental.pallas import tpu as pltpu

F32 = jnp.float32
BF16 = jnp.bfloat16

N_HEADS = 8
N_RNN_BLOCKS = 8
CONV_W = 4
LRU_C = 8.0
N_GROUPS = 4
EXPERTS_PER_GROUP = 8
N_EXPERTS = N_GROUPS * EXPERTS_PER_GROUP
GROUP_LANE = N_EXPERTS
LN_EPS = 1e-5
LANES = 128
SUBLANES = 8
MXU_N = 256
VMEM_LIMIT = 56 * 1024 * 1024

NEG_INF = float("-inf")
LOG2_E = math.log2(math.e)


def _cparams(sem):
    return pltpu.CompilerParams(dimension_semantics=sem, vmem_limit_bytes=VMEM_LIMIT)


def _split3(x):
    hi = x.astype(BF16)
    r1 = x - hi.astype(F32)
    mid = r1.astype(BF16)
    lo = (r1 - mid.astype(F32)).astype(BF16)
    return hi, mid, lo


def _dot(a, b):
    return jnp.dot(a, b, preferred_element_type=F32)


def _layer_norm(z, g, b):
    mu = jnp.mean(z, axis=-1, keepdims=True)
    zc = z - mu
    var = jnp.mean(zc * zc, axis=-1, keepdims=True)
    return zc * lax.rsqrt(var + LN_EPS) * g + b


def _log_sigmoid(x):
    return jnp.minimum(x, 0.0) - jnp.log1p(jnp.exp(-jnp.abs(x)))


def _softplus(x):
    return jnp.maximum(x, 0.0) + jnp.log1p(jnp.exp(-jnp.abs(x)))


def _gelu_tanh(x):
    c = math.sqrt(2.0 / math.pi)
    return 0.5 * x * (1.0 + jnp.tanh(c * (x + 0.044715 * (x * x * x))))


def _mod_spec(per_row, tm, d, chunk, tiles_per_batch):
    if per_row:
        return pl.BlockSpec((None, tm, d), lambda i: (0, i, chunk))
    return pl.BlockSpec((None, 1, d), lambda i: (i // tiles_per_batch, 0, chunk))


def _mods_kernel(c_ref, w_ref, b_ref, o_ref):
    c = c_ref[...]
    s = (c * jax.nn.sigmoid(c)).astype(BF16)
    o_ref[...] = _dot(s, w_ref[...].astype(BF16)) + b_ref[...]


def _mods_call(c_all, w_mod, b_mod):
    rc, d = c_all.shape
    n = w_mod.shape[1]
    tn = 1536
    return pl.pallas_call(
        _mods_kernel,
        grid=(n // tn,),
        in_specs=[pl.BlockSpec((rc, d), lambda j: (0, 0)),
                  pl.BlockSpec((d, tn), lambda j: (0, j)),
                  pl.BlockSpec((1, tn), lambda j: (0, j))],
        out_specs=pl.BlockSpec((rc, tn), lambda j: (0, j)),
        out_shape=jax.ShapeDtypeStruct((rc, n), F32),
        compiler_params=_cparams(("arbitrary",)),
        name="adaln_mods",
    )(c_all, w_mod, b_mod.reshape(1, n))


def _inproj_kernel(x_ref, shift_ref, scale_ref, w_ref, bf_ref, tri_ref,
                   xb_ref, gb_ref, q_ref, k_ref, v_ref, kb_ref, vb_ref, gl_ref, ga_ref, logf_ref, cum_ref,
                   carry_ref, *, tiles_per_batch, q_scale):
    i = pl.program_id(0)
    d = x_ref.shape[1]
    tm = x_ref.shape[0]
    u = (x_ref[...] * (1.0 + scale_ref[...]) + shift_ref[...]).astype(BF16)

    def mm(c):
        return _dot(u, w_ref[:, c * d:(c + 1) * d])

    xb_ref[...] = mm(0)
    gb_ref[...] = mm(1)
    q_ref[...] = (mm(2) * q_scale).astype(BF16)
    k = mm(3)
    k_ref[...] = k
    kb_ref[...] = k.astype(BF16)
    v = mm(4)
    v_ref[...] = v
    vb_ref[...] = v.astype(BF16)
    gl_ref[...] = mm(5)
    ga_ref[...] = mm(6)
    fg = _dot(u, w_ref[:, 7 * d:7 * d + LANES]) + bf_ref[...]
    logf = _log_sigmoid(fg)
    logf_ref[...] = logf[:, :N_HEADS]

    @pl.when(i % tiles_per_batch == 0)
    def _():
        carry_ref[...] = jnp.zeros_like(carry_ref)

    tri = tri_ref[...]
    hi, mid, lo = _split3(logf)
    cum = _dot(tri, hi) + _dot(tri, mid) + _dot(tri, lo) + carry_ref[...]
    cum_ref[...] = cum[:, :N_HEADS]
    carry_ref[...] = cum[tm - 1:tm, :]


def _inproj_call(x, mods, w_cat, bf_pad, *, per_row, rows_per_batch, q_scale):
    r, d = x.shape
    tm = min(256, r)
    tiles_per_batch = max(rows_per_batch // tm, 1)
    tri = (jnp.arange(tm)[:, None] >= jnp.arange(tm)[None, :]).astype(BF16)
    row_f32 = pl.BlockSpec((tm, d), lambda i: (i, 0))
    small = pl.BlockSpec((tm, N_HEADS), lambda i: (i, 0))
    sd = lambda dt: jax.ShapeDtypeStruct((r, d), dt)
    sh = jax.ShapeDtypeStruct((r, N_HEADS), F32)
    return pl.pallas_call(
        functools.partial(_inproj_kernel, tiles_per_batch=tiles_per_batch, q_scale=q_scale),
        grid=(r // tm,),
        in_specs=[row_f32,
                  _mod_spec(per_row, tm, d, 0, tiles_per_batch),
                  _mod_spec(per_row, tm, d, 1, tiles_per_batch),
                  pl.BlockSpec(w_cat.shape, lambda i: (0, 0), pipeline_mode=pl.Buffered(1)),
                  pl.BlockSpec((1, LANES), lambda i: (0, 0)),
                  pl.BlockSpec((tm, tm), lambda i: (0, 0))],
        out_specs=[row_f32, row_f32, row_f32, row_f32, row_f32, row_f32, row_f32, row_f32, row_f32, small, small],
        out_shape=[sd(F32), sd(F32), sd(BF16), sd(F32), sd(F32), sd(BF16), sd(BF16), sd(F32), sd(F32), sh, sh],
        scratch_shapes=[pltpu.VMEM((1, LANES), F32)],
        compiler_params=_cparams(("arbitrary",)),
        name="in_projection",
    )(x, mods, mods, w_cat, bf_pad, tri)


def _lru_gates(xc, wrg_ref, brg, wig_ref, big, lam):
    blk = xc.shape[1] // N_RNN_BLOCKS
    rs, is_ = [], []
    for n in range(N_RNN_BLOCKS):
        xn = xc[:, n * blk:(n + 1) * blk].astype(BF16)
        rs.append(_dot(xn, wrg_ref[n]))
        is_.append(_dot(xn, wig_ref[n]))
    r = jax.nn.sigmoid(jnp.concatenate(rs, axis=1) + brg)
    ig = jax.nn.sigmoid(jnp.concatenate(is_, axis=1) + big)
    log_a = (-LRU_C) * r * _softplus(-lam)
    a = jnp.exp(log_a)
    mult = jnp.sqrt(1.0 - jnp.exp(2.0 * log_a))
    return a, mult, ig


def _lru_prompt_kernel(xb_ref, gb_ref, cw_ref, cb_ref, wrg_ref, brg_ref, wig_ref, big_ref, lam_ref,
                       out_ref, hlast_ref, xp_ref, a_ref, b_ref, hs_ref, h_ref):
    t = pl.program_id(1)
    nt = pl.num_programs(1)
    tm = xb_ref.shape[0]
    halo = SUBLANES

    @pl.when(t == 0)
    def _():
        xp_ref[0:halo, :] = jnp.zeros((halo, xp_ref.shape[1]), F32)
        h_ref[...] = jnp.zeros_like(h_ref)

    xp_ref[halo:halo + tm, :] = xb_ref[...]
    xc = cb_ref[...]
    for tap in range(CONV_W):
        off = halo - (CONV_W - 1) + tap
        xc = xc + xp_ref[off:off + tm, :] * cw_ref[tap:tap + 1, :]
    a, mult, ig = _lru_gates(xc, wrg_ref, brg_ref[...], wig_ref, big_ref[...], lam_ref[...])
    row = lax.broadcasted_iota(jnp.int32, (tm, 1), 0)
    mult = jnp.where((row == 0) & (t == 0), 1.0, mult)
    a_ref[...] = a
    b_ref[...] = mult * ig * xc

    def step(s, h):
        h = a_ref[pl.ds(s, 1), :] * h + b_ref[pl.ds(s, 1), :]
        hs_ref[pl.ds(s, 1), :] = h
        return h

    h = lax.fori_loop(0, tm, step, h_ref[...], unroll=8)
    h_ref[...] = h
    xp_ref[halo - (CONV_W - 1):halo, :] = xp_ref[halo + tm - (CONV_W - 1):halo + tm, :]
    out_ref[...] = (hs_ref[...] * _gelu_tanh(gb_ref[...])).astype(BF16)

    @pl.when(t == nt - 1)
    def _():
        hlast_ref[...] = h


def _lru_prompt_call(xb, gb, b, t_len, lw):
    r, d = xb.shape
    tm = min(256, t_len)
    nt = t_len // tm
    row = pl.BlockSpec((tm, d), lambda bi, ti: (bi * nt + ti, 0))
    vec = pl.BlockSpec((1, d), lambda bi, ti: (0, 0))
    blk = d // N_RNN_BLOCKS
    wspec = pl.BlockSpec((N_RNN_BLOCKS, blk, blk), lambda bi, ti: (0, 0, 0))
    return pl.pallas_call(
        _lru_prompt_kernel,
        grid=(b, nt),
        in_specs=[row, row, pl.BlockSpec((CONV_W, d), lambda bi, ti: (0, 0)), vec, wspec, vec, wspec, vec, vec],
        out_specs=[row, pl.BlockSpec((None, 1, d), lambda bi, ti: (bi, 0, 0))],
        out_shape=[jax.ShapeDtypeStruct((r, d), BF16), jax.ShapeDtypeStruct((b, 1, d), F32)],
        scratch_shapes=[pltpu.VMEM((tm + SUBLANES, d), F32), pltpu.VMEM((tm, d), F32), pltpu.VMEM((tm, d), F32),
                        pltpu.VMEM((tm, d), F32), pltpu.VMEM((1, d), F32)],
        compiler_params=_cparams(("arbitrary", "arbitrary")),
        name="rglru_prompt",
    )(xb, gb, lw["conv_w"], lw["conv_b"], lw["w_rg"], lw["b_rg"], lw["w_ig"], lw["b_ig"], lw["lam"])


def _lru_sample_kernel(xb_ref, gb_ref, c0_ref, c1_ref, c2_ref, hprev_ref, cw_ref, cb_ref,
                       wrg_ref, brg_ref, wig_ref, big_ref, lam_ref, out_ref, hnew_ref):
    xb = xb_ref[...]
    xc = (cb_ref[...] + c0_ref[...] * cw_ref[0:1, :] + c1_ref[...] * cw_ref[1:2, :]
          + c2_ref[...] * cw_ref[2:3, :] + xb * cw_ref[3:4, :])
    a, mult, ig = _lru_gates(xc, wrg_ref, brg_ref[...], wig_ref, big_ref[...], lam_ref[...])
    h = a * hprev_ref[...] + mult * ig * xc
    hnew_ref[...] = h
    out_ref[...] = (h * _gelu_tanh(gb_ref[...])).astype(BF16)


def _lru_sample_call(xb, gb, conv_taps, h_prev, lw):
    r, d = xb.shape
    return pl.pallas_call(
        _lru_sample_kernel,
        out_shape=[jax.ShapeDtypeStruct((r, d), BF16), jax.ShapeDtypeStruct((r, d), F32)],
        compiler_params=pltpu.CompilerParams(vmem_limit_bytes=VMEM_LIMIT),
        name="rglru_sample",
    )(xb, gb, conv_taps[0], conv_taps[1], conv_taps[2], h_prev, lw["conv_w"], lw["conv_b"],
      lw["w_rg"], lw["b_rg"], lw["w_ig"], lw["b_ig"], lw["lam"])


def _attn_prompt_kernel(qi_ref, ki_ref, q_ref, k_ref, v_ref, fk_ref, o_ref, m_ref, acc_ref):
    p_id = pl.program_id(1)
    qi = qi_ref[p_id]
    ki = ki_ref[p_id]
    tq, tk = q_ref.shape[0], k_ref.shape[0]
    hd = q_ref.shape[1] // N_HEADS

    @pl.when(ki == 0)
    def _():
        m_ref[...] = jnp.full_like(m_ref, NEG_INF)
        acc_ref[...] = jnp.zeros_like(acc_ref)

    def update(masked):
        ones = jnp.ones((tk, hd), BF16)
        if masked:
            keep = (lax.broadcasted_iota(jnp.int32, (tq, tk), 0) >= lax.broadcasted_iota(jnp.int32, (tq, tk), 1))
        for h in range(N_HEADS):
            sl = slice(h * hd, (h + 1) * hd)
            s = lax.dot_general(q_ref[:, sl], k_ref[:, sl], (((1,), (1,)), ((), ())), preferred_element_type=F32)
            s = s - fk_ref[h:h + 1, :] * LOG2_E
            if masked:
                s = jnp.where(keep, s, NEG_INF)
            m_prev = m_ref[h]
            m_new = jnp.maximum(m_prev, jnp.max(s, axis=-1, keepdims=True))
            p = jnp.exp2(s - m_new).astype(BF16)
            pv = _dot(p, jnp.concatenate([v_ref[:, sl], ones], axis=1))
            acc_ref[h] = jnp.exp2(m_prev - m_new) * acc_ref[h] + pv
            m_ref[h] = m_new

    @pl.when(ki < qi)
    def _():
        update(False)

    @pl.when(ki == qi)
    def _():
        update(True)
        for h in range(N_HEADS):
            a = acc_ref[h]
            o_ref[:, h * hd:(h + 1) * hd] = (a[:, :hd] / a[:, hd:]).astype(o_ref.dtype)


def _attn_prompt_call(q, kb, vb, cum, b, t_len):
    r, d = q.shape
    hd = d // N_HEADS
    tq = min(512, t_len)
    nq = t_len // tq
    pairs = [(a, c) for a in range(nq) for c in range(a + 1)]
    qi_tab = jnp.asarray([p[0] for p in pairs], jnp.int32)
    ki_tab = jnp.asarray([p[1] for p in pairs], jnp.int32)
    fk = jnp.transpose(cum.reshape(b, t_len, N_HEADS), (0, 2, 1))
    grid_spec = pltpu.PrefetchScalarGridSpec(
        num_scalar_prefetch=2,
        grid=(b, len(pairs)),
        in_specs=[pl.BlockSpec((tq, d), lambda bi, p, qt, kt: (bi * nq + qt[p], 0)),
                  pl.BlockSpec((tq, d), lambda bi, p, qt, kt: (bi * nq + kt[p], 0)),
                  pl.BlockSpec((tq, d), lambda bi, p, qt, kt: (bi * nq + kt[p], 0)),
                  pl.BlockSpec((None, N_HEADS, tq), lambda bi, p, qt, kt: (bi, 0, kt[p]))],
        out_specs=pl.BlockSpec((tq, d), lambda bi, p, qt, kt: (bi * nq + qt[p], 0)),
        scratch_shapes=[pltpu.VMEM((N_HEADS, tq, 1), F32), pltpu.VMEM((N_HEADS, tq, 2 * hd), F32)],
    )
    return pl.pallas_call(
        _attn_prompt_kernel,
        grid_spec=grid_spec,
        out_shape=jax.ShapeDtypeStruct((r, d), BF16),
        compiler_params=_cparams(("arbitrary", "arbitrary")),
        name="fox_attention_prompt",
    )(qi_tab, ki_tab, q, kb, vb, fk)


def _logf_pages_kernel(l_ref, u_ref, ones_ref, pe_ref, po_ref, c_ref, tot_ref):
    parts = _split3(l_ref[...])
    insuf = sum(_dot(p, u_ref[...]) for p in parts) * LOG2_E
    tot_ref[...] = sum(_dot(p, ones_ref[...]) for p in parts) * LOG2_E
    hi = insuf.astype(BF16)
    mid = (insuf - hi.astype(F32)).astype(BF16)
    c_ref[...] = _dot(hi, pe_ref[...]) + _dot(mid, po_ref[...])


def _logf_pages_call(cache_logf):
    n_pool, page, h = cache_logf.shape
    rows = n_pool * h
    lft = jnp.swapaxes(cache_logf, 1, 2).reshape(rows, page)
    kk = jnp.arange(page)
    later = (kk[:, None] > kk[None, :]).astype(BF16)
    ones = jnp.ones((page, page), BF16)
    pe = (2 * kk[:, None] == jnp.arange(2 * page)[None, :]).astype(BF16)
    po = (2 * kk[:, None] + 1 == jnp.arange(2 * page)[None, :]).astype(BF16)
    tr = 2048
    while rows % tr:
        tr //= 2
    const = lambda shape: pl.BlockSpec(shape, lambda i: (0, 0))
    c, tot = pl.pallas_call(
        _logf_pages_kernel,
        grid=(rows // tr,),
        in_specs=[pl.BlockSpec((tr, page), lambda i: (i, 0)), const((page, page)), const((page, page)),
                  const((page, 2 * page)), const((page, 2 * page))],
        out_specs=[pl.BlockSpec((tr, 2 * page), lambda i: (i, 0)), pl.BlockSpec((tr, page), lambda i: (i, 0))],
        out_shape=[jax.ShapeDtypeStruct((rows, 2 * page), F32), jax.ShapeDtypeStruct((rows, page), F32)],
        compiler_params=_cparams(("arbitrary",)),
        name="logf_page_sums",
    )(lft, later, ones, pe, po)
    return c.reshape(n_pool, h, 2 * page), tot.reshape(n_pool, h, page)


def _attn_sample_kernel(pt_ref, q_ref, knew_ref, vnew_ref, cq_ref, ones_ref, *refs, pages_per_step):
    pp = pages_per_step
    k_refs, v_refs = refs[0:pp], refs[pp:2 * pp]
    c_refs, tot_refs = refs[2 * pp:3 * pp], refs[3 * pp:4 * pp]
    o_ref = refs[4 * pp]
    m_ref, l_ref, acc_ref, psuf_ref = refs[4 * pp + 1:]
    j = pl.program_id(1)
    nj = pl.num_programs(1)
    page, h, hd = k_refs[0].shape
    q = q_ref[...]

    def lane_sums(lhs):
        return _dot(lhs.astype(BF16), ones_ref[...])

    @pl.when(j == 0)
    def _():
        prod = jnp.concatenate([knew_ref[...] * q, jnp.zeros((h, hd), F32)], axis=1)
        prod = jnp.concatenate([prod, jnp.zeros_like(prod)], axis=0)
        m_ref[...] = lane_sums(prod)[:h]
        l_ref[...] = jnp.ones_like(l_ref)
        acc_ref[...] = vnew_ref[...]
        psuf_ref[...] = cq_ref[...]

    half = page // 2
    key = lax.broadcasted_iota(jnp.int32, (half, h, hd), 0)
    lane = lax.broadcasted_iota(jnp.int32, (half, h, hd), 2)
    own = (lane >> 1) == key
    scores, psufs = [], []
    psuf = psuf_ref[...]
    m_new = m_ref[...]
    for i in range(pp):
        prod = k_refs[i][...] * q[None]
        c = c_refs[i][...]
        bias = jnp.concatenate([jnp.where(own, c[None, :, :hd], 0.0), jnp.where(own, c[None, :, hd:], 0.0)], axis=0)
        lhs = jnp.concatenate([prod, bias], axis=2).reshape(page * h, 2 * hd)
        s = lane_sums(lhs).reshape(page, h, hd)
        m_new = jnp.maximum(m_new, jnp.max(s, axis=0) + psuf)
        scores.append(s)
        psufs.append(psuf)
        psuf = psuf + tot_refs[i][...]
    corr = jnp.exp2(m_ref[...] - m_new)
    l_new = l_ref[...] * corr
    acc = acc_ref[...] * corr
    for i in range(pp):
        p = jnp.exp2(scores[i] - (m_new - psufs[i])[None])
        l_new = l_new + jnp.sum(p, axis=0)
        acc = acc + jnp.sum(p * v_refs[i][...], axis=0)
    l_ref[...] = l_new
    acc_ref[...] = acc
    m_ref[...] = m_new
    psuf_ref[...] = psuf

    @pl.when(j == nj - 1)
    def _():
        o_ref[...] = acc_ref[...] / l_ref[...]


def _attn_sample_call(q, k_new, v_new, logf_new, cache_k, cache_v, c_pages, tot_pages, page_table):
    db, d = q.shape
    n_pool, page, h, hd = cache_k.shape
    assert page == hd, "the two bias terms of a page's keys are packed into 2*HEAD_DIM lanes"
    n_pages = page_table.shape[1]
    pp = 8
    while n_pages % pp:
        pp //= 2
    n_steps = n_pages // pp
    cq = jnp.broadcast_to((logf_new * LOG2_E)[:, :, None], (db, h, hd))
    ones = jnp.ones((2 * hd, hd), BF16)

    def page_of(i):
        return lambda bi, j, pt: (pt[bi, n_pages - 1 - (j * pp + i)], 0, 0)

    def page4_of(i):
        return lambda bi, j, pt: (pt[bi, n_pages - 1 - (j * pp + i)], 0, 0, 0)

    tile = pl.BlockSpec((None, h, hd), lambda bi, j, pt: (bi, 0, 0))
    in_specs = [tile, tile, tile, tile, pl.BlockSpec((2 * hd, hd), lambda bi, j, pt: (0, 0))]
    in_specs += [pl.BlockSpec((None, page, h, hd), page4_of(i)) for i in range(pp)]
    in_specs += [pl.BlockSpec((None, page, h, hd), page4_of(i)) for i in range(pp)]
    in_specs += [pl.BlockSpec((None, h, 2 * hd), page_of(i)) for i in range(pp)]
    in_specs += [pl.BlockSpec((None, h, hd), page_of(i)) for i in range(pp)]
    grid_spec = pltpu.PrefetchScalarGridSpec(
        num_scalar_prefetch=1,
        grid=(db, n_steps),
        in_specs=in_specs,
        out_specs=tile,
        scratch_shapes=[pltpu.VMEM((h, hd), F32)] * 4,
    )
    out = pl.pallas_call(
        functools.partial(_attn_sample_kernel, pages_per_step=pp),
        grid_spec=grid_spec,
        out_shape=jax.ShapeDtypeStruct((db, h, hd), F32),
        compiler_params=_cparams(("arbitrary", "arbitrary")),
        name="fox_attention_sample",
    )(page_table, q.astype(F32).reshape(db, h, hd), k_new.reshape(db, h, hd), v_new.reshape(db, h, hd), cq, ones,
      *([cache_k] * pp), *([cache_v] * pp), *([c_pages] * pp), *([tot_pages] * pp))
    return out.reshape(db, d)


def _router_gates(logits):
    lane = lax.broadcasted_iota(jnp.int32, logits.shape, 1)
    lane_f = lane.astype(F32)
    big = float(4 * LANES)
    is_grp = (lane >= N_EXPERTS) & (lane < N_EXPERTS + N_GROUPS)
    gl = jnp.where(is_grp, logits, NEG_INF)
    gmax = jnp.max(gl, axis=-1, keepdims=True)
    gidx = jnp.min(jnp.where(gl == gmax, lane_f - N_EXPERTS, big), axis=-1, keepdims=True)
    g_p = 1.0 / jnp.sum(jnp.exp(gl - gmax), axis=-1, keepdims=True)
    in_grp = (lane < N_EXPERTS) & ((lane // EXPERTS_PER_GROUP).astype(F32) == gidx)
    e1 = jnp.where(in_grp, logits, NEG_INF)
    m1 = jnp.max(e1, axis=-1, keepdims=True)
    i1 = jnp.min(jnp.where(e1 == m1, lane_f, big), axis=-1, keepdims=True)
    e2 = jnp.where(lane_f == i1, NEG_INF, e1)
    m2 = jnp.max(e2, axis=-1, keepdims=True)
    i2 = jnp.min(jnp.where(e2 == m2, lane_f, big), axis=-1, keepdims=True)
    dlt = jnp.exp(m2 - m1)
    w1 = g_p / (1.0 + dlt)
    w2 = g_p * dlt / (1.0 + dlt)
    gates = jnp.where(lane_f == i1, w1, 0.0) + jnp.where(lane_f == i2, w2, 0.0)
    return jnp.where(lane == GROUP_LANE, gidx, gates)


def _merge_kernel(x_ref, lru_ref, att_ref, gl_ref, ga_ref, gate1_ref, shift2_ref, scale2_ref,
                  wl_ref, wa_ref, wo_ref, g_ref, b_ref, wr_ref, br_ref,
                  x1_ref, u2_ref, gates_ref, *, alpha):
    y = (jax.nn.sigmoid(gl_ref[...]) * _dot(lru_ref[...], wl_ref[...])
         + jax.nn.sigmoid(ga_ref[...]) * _dot(att_ref[...], wa_ref[...]))
    mix = _dot(y.astype(BF16), wo_ref[...])
    x1 = _layer_norm(alpha * x_ref[...] + gate1_ref[...] * mix, g_ref[...], b_ref[...])
    x1_ref[...] = x1
    u2 = x1 * (1.0 + scale2_ref[...]) + shift2_ref[...]
    u2_ref[...] = u2.astype(BF16)
    logits = jnp.dot(u2, wr_ref[...], preferred_element_type=F32, precision=lax.Precision.HIGHEST) + br_ref[...]
    gates_ref[...] = _router_gates(logits)


def _merge_call(x, lru, att, g_lru, g_att, mods, lw, *, per_row, rows_per_batch, alpha):
    r, d = x.shape
    tm = min(256, r)
    tpb = max(rows_per_batch // tm, 1)
    row = pl.BlockSpec((tm, d), lambda i: (i, 0))
    wsq = pl.BlockSpec((d, d), lambda i: (0, 0))
    vec = pl.BlockSpec((1, d), lambda i: (0, 0))
    return pl.pallas_call(
        functools.partial(_merge_kernel, alpha=alpha),
        grid=(r // tm,),
        in_specs=[row, row, row, row, row,
                  _mod_spec(per_row, tm, d, 2, tpb), _mod_spec(per_row, tm, d, 3, tpb), _mod_spec(per_row, tm, d, 4, tpb),
                  wsq, wsq, wsq, vec, vec,
                  pl.BlockSpec((d, LANES), lambda i: (0, 0)), pl.BlockSpec((1, LANES), lambda i: (0, 0))],
        out_specs=[row, row, pl.BlockSpec((tm, LANES), lambda i: (i, 0))],
        out_shape=[jax.ShapeDtypeStruct((r, d), F32), jax.ShapeDtypeStruct((r, d), BF16),
                   jax.ShapeDtypeStruct((r, LANES), F32)],
        compiler_params=_cparams(("arbitrary",)),
        name="merge_ln_router",
    )(x, lru, att, g_lru, g_att, mods, mods, mods, lw["w_br_lru"], lw["w_br_attn"], lw["w_out"],
      lw["ln1_g"], lw["ln1_b"], lw["w_router"], lw["b_router"])


def _moe_kernel(u_ref, route_ref, wg_ref, wu_ref, wd_ref, x1_ref, gate2_ref, g_ref, b_ref, y_ref,
                xs_ref, gs_ref, ys_ref, pos_ref, off_ref, *, alpha, chunk):
    g = pl.program_id(1)
    ng = pl.num_programs(1)
    t, d = u_ref.shape
    n_chunks = t // chunk
    epg = wg_ref.shape[0]
    lane = lax.broadcasted_iota(jnp.int32, (chunk, LANES), 1)

    @pl.when(g == 0)
    def _():
        tri = jnp.where(lax.broadcasted_iota(jnp.int32, (chunk, chunk), 0)
                        >= lax.broadcasted_iota(jnp.int32, (chunk, chunk), 1), 1.0, 0.0).astype(BF16)
        carry = jnp.zeros((1, LANES), F32)
        seen = []
        for c in range(n_chunks):
            grp = route_ref[c * chunk:(c + 1) * chunk, GROUP_LANE:GROUP_LANE + 1]
            mine = lane.astype(F32) == grp
            cum = _dot(tri, jnp.where(mine, 1.0, 0.0).astype(BF16)) + carry
            carry = cum[chunk - 1:chunk, :]
            seen.append((mine, cum))
        counts = jnp.broadcast_to(carry, (SUBLANES, LANES))
        lane8 = lax.broadcasted_iota(jnp.int32, (SUBLANES, LANES), 1)
        off = jnp.zeros((SUBLANES, LANES), F32)
        for sft in range(1, ng):
            off = off + jnp.where(lane8 >= sft, pltpu.roll(counts, sft, axis=1), 0.0)
        for c, (mine, cum) in enumerate(seen):
            pos = jnp.sum(jnp.where(mine, off[0:1, :] + cum - 1.0, 0.0), axis=-1, keepdims=True)
            pos_ref[c * chunk:(c + 1) * chunk, :] = jnp.broadcast_to(pos, (chunk, LANES))
        for gg in range(ng):
            off_ref[gg] = jnp.sum(jnp.where(lane8[0:1] == gg, off[0:1], 0.0)).astype(jnp.int32)
        off_ref[ng] = t
        pos_row = pos_ref[...].T[0:1, :]
        u = u_ref[...]
        r_parts = _split3(route_ref[...])
        for c in range(n_chunks):
            rows = slice(c * chunk, (c + 1) * chunk)
            dst = (lax.broadcasted_iota(jnp.int32, (chunk, t), 0) + c * chunk).astype(F32)
            pm = jnp.where(dst == pos_row, 1.0, 0.0).astype(BF16)
            xs_ref[rows, :] = _dot(pm, u).astype(BF16)
            gs_ref[rows, :] = sum(_dot(pm, p) for p in r_parts)
        ys_ref[...] = jnp.zeros_like(ys_ref)

    c_lo = off_ref[g] // chunk
    c_hi = (off_ref[g + 1] + chunk - 1) // chunk

    def run_chunk(c, carry):
        r0 = pl.multiple_of(c * chunk, chunk)
        xc = xs_ref[pl.ds(r0, chunk), :]
        gc = gs_ref[pl.ds(r0, chunk), :]
        acc = jnp.zeros((chunk, d), F32)
        for e in range(epg):
            ge = jnp.sum(jnp.where(lane == g * epg + e, gc, 0.0), axis=-1, keepdims=True)
            hg = _dot(xc, wg_ref[e])
            hid = hg * jax.nn.sigmoid(hg) * _dot(xc, wu_ref[e]) * ge
            acc = acc + _dot(hid.astype(BF16), wd_ref[e])
        ys_ref[pl.ds(r0, chunk), :] += acc
        return carry

    lax.fori_loop(c_lo, c_hi, run_chunk, 0)

    @pl.when(g == ng - 1)
    def _():
        ys = ys_ref[...]
        y_hi = ys.astype(BF16)
        y_lo = (ys - y_hi.astype(F32)).astype(BF16)
        gate2 = gate2_ref[...]
        for c in range(n_chunks):
            rows = slice(c * chunk, (c + 1) * chunk)
            src = lax.broadcasted_iota(jnp.int32, (chunk, t), 1).astype(F32)
            pm = jnp.where(src == pos_ref[rows, 0:1], 1.0, 0.0).astype(BF16)
            ffn = _dot(pm, y_hi) + _dot(pm, y_lo)
            g2 = gate2 if gate2.shape[0] == 1 else gate2[rows, :]
            y_ref[rows, :] = _layer_norm(alpha * x1_ref[rows, :] + g2 * ffn, g_ref[...], b_ref[...])


def _moe_call(u2, route, x1, mods, lw, *, per_row, rows_per_batch, alpha):
    r, d = x1.shape
    ng, epg, _, f = lw["w_gate"].shape
    tm = min(1024, r)
    chunk = min(128, tm)
    tpb = max(rows_per_batch // tm, 1)
    once = lambda w: pl.BlockSpec((tm, w), lambda i, g: (i, 0), pipeline_mode=pl.Buffered(1))
    vec = pl.BlockSpec((1, d), lambda i, g: (0, 0))
    if per_row:
        gate2 = pl.BlockSpec((None, tm, d), lambda i, g: (0, i, 5))
    else:
        gate2 = pl.BlockSpec((None, 1, d), lambda i, g: (i // tpb, 0, 5))
    return pl.pallas_call(
        functools.partial(_moe_kernel, alpha=alpha, chunk=chunk),
        grid=(r // tm, ng),
        in_specs=[once(d), once(LANES),
                  pl.BlockSpec((None, epg, d, f), lambda i, g: (g, 0, 0, 0)),
                  pl.BlockSpec((None, epg, d, f), lambda i, g: (g, 0, 0, 0)),
                  pl.BlockSpec((None, epg, f, d), lambda i, g: (g, 0, 0, 0)),
                  once(d), gate2, vec, vec],
        out_specs=pl.BlockSpec((tm, d), lambda i, g: (i, 0)),
        out_shape=jax.ShapeDtypeStruct((r, d), F32),
        scratch_shapes=[pltpu.VMEM((tm, d), BF16), pltpu.VMEM((tm, LANES), F32), pltpu.VMEM((tm, d), F32),
                        pltpu.VMEM((tm, LANES), F32), pltpu.SMEM((ng + 1,), jnp.int32)],
        compiler_params=_cparams(("arbitrary", "arbitrary")),
        name="moe_ffn_ln",
    )(u2, route, lw["w_gate"], lw["w_up"], lw["w_down"], x1, mods, lw["ln2_g"], lw["ln2_b"])


def _prep_layer_weights(li, w_in, b_f, conv_w, conv_b, w_rg, b_rg, w_ig, b_ig, lru_lambda, w_br_lru, w_br_attn,
                        w_out, ln1_g, ln1_b, w_grp, b_grp, w_rt, b_rt, w_gate, w_up, w_down, ln2_g, ln2_b):
    d = w_in.shape[1]
    h = b_f.shape[1]
    wi = w_in[li]
    c_main = 5 * d
    w_f = jnp.pad(wi[:, c_main:c_main + h], ((0, 0), (0, LANES - h)))
    w_cat = jnp.concatenate([wi[:, :c_main], wi[:, c_main + h:], w_f], axis=1).astype(BF16)
    w_router = jnp.pad(jnp.concatenate([w_rt[li], w_grp[li]], axis=1), ((0, 0), (0, LANES - N_EXPERTS - N_GROUPS)))
    b_router = jnp.pad(jnp.concatenate([b_rt[li], b_grp[li]]), (0, LANES - N_EXPERTS - N_GROUPS)).reshape(1, LANES)
    row = lambda v: v.reshape(1, -1)
    by_group = lambda w: w.astype(BF16).reshape(N_GROUPS, EXPERTS_PER_GROUP, *w.shape[1:])
    return {
        "w_cat": w_cat, "bf_pad": jnp.pad(b_f[li], (0, LANES - h)).reshape(1, LANES),
        "conv_w": conv_w[li], "conv_b": row(conv_b[li]),
        "w_rg": w_rg[li].astype(BF16), "b_rg": row(b_rg[li]), "w_ig": w_ig[li].astype(BF16), "b_ig": row(b_ig[li]),
        "lam": row(lru_lambda[li]),
        "w_br_lru": w_br_lru[li].astype(BF16), "w_br_attn": w_br_attn[li].astype(BF16), "w_out": w_out[li].astype(BF16),
        "ln1_g": row(ln1_g[li]), "ln1_b": row(ln1_b[li]), "ln2_g": row(ln2_g[li]), "ln2_b": row(ln2_b[li]),
        "w_router": w_router, "b_router": b_router,
        "w_gate": by_group(w_gate[li]), "w_up": by_group(w_up[li]), "w_down": by_group(w_down[li]),
    }


def kernel(x_prompt, x_sample, cache_k, cache_v, cache_logf, state_conv, state_h, page_table, c_prompt, c_sample, w_mod, b_mod, w_in, b_f, conv_w, conv_b, w_rg, b_rg, w_ig, b_ig, lru_lambda, w_br_lru, w_br_attn, w_out, ln1_g, ln1_b, w_grp, b_grp, w_rt, b_rt, w_gate, w_up, w_down, ln2_g, ln2_b):
    b, t_len, d = x_prompt.shape
    db, s_len, _ = x_sample.shape
    depth = w_mod.shape[0]
    h = N_HEADS
    hd = d // h
    assert s_len == 1, "the sample group decodes one token per sequence"
    assert t_len >= CONV_W - 1
    alpha = (2.0 * depth) ** 0.25
    q_scale = hd ** -0.5 * LOG2_E

    xp = x_prompt.reshape(b * t_len, d)
    xs = x_sample.reshape(db, d)
    rc = -(-(b + db) // SUBLANES) * SUBLANES
    c_all = jnp.concatenate([c_prompt, c_sample, jnp.zeros((rc - b - db, d), F32)], axis=0)
    st_p, st_s = [], []
    for li in range(depth):
        lw = _prep_layer_weights(li, w_in, b_f, conv_w, conv_b, w_rg, b_rg, w_ig, b_ig, lru_lambda, w_br_lru,
                                 w_br_attn, w_out, ln1_g, ln1_b, w_grp, b_grp, w_rt, b_rt, w_gate, w_up, w_down,
                                 ln2_g, ln2_b)
        mods = _mods_call(c_all, w_mod[li], b_mod[li])
        mods_p = mods[:b].reshape(b, 1, 6 * d)
        mods_s = mods[b:b + db].reshape(1, db, 6 * d)

        xb, gb, q, k, v, kb, vb, g_lru, g_att, logf, cum = _inproj_call(
            xp, mods_p, lw["w_cat"], lw["bf_pad"], per_row=False, rows_per_batch=t_len, q_scale=q_scale)
        lru, h_last = _lru_prompt_call(xb, gb, b, t_len, lw)
        att = _attn_prompt_call(q, kb, vb, cum, b, t_len)
        x1, u2, gates = _merge_call(xp, lru, att, g_lru, g_att, mods_p, lw, per_row=False, rows_per_batch=t_len, alpha=alpha)
        xp = _moe_call(u2, gates, x1, mods_p, lw, per_row=False, rows_per_batch=t_len, alpha=alpha)
        st_p.append((k.reshape(b, t_len, h, hd), v.reshape(b, t_len, h, hd), logf.reshape(b, t_len, h),
                     xb.reshape(b, t_len, d)[:, t_len - (CONV_W - 1):], h_last.reshape(b, d)))

        xb_s, gb_s, q_s, k_s, v_s, _, _, gl_s, ga_s, logf_s, _ = _inproj_call(
            xs, mods_s, lw["w_cat"], lw["bf_pad"], per_row=True, rows_per_batch=1, q_scale=q_scale)
        taps = [state_conv[li][:, tap, :] for tap in range(CONV_W - 1)]
        lru_s, h_new = _lru_sample_call(xb_s, gb_s, taps, state_h[li], lw)
        b3_pages, tot_pages = _logf_pages_call(cache_logf[li])
        att_s = _attn_sample_call(q_s, k_s, v_s, logf_s, cache_k[li], cache_v[li], b3_pages, tot_pages, page_table)
        x1_s, u2_s, gates_s = _merge_call(xs, lru_s, att_s.astype(BF16), gl_s, ga_s, mods_s, lw,
                                          per_row=True, rows_per_batch=1, alpha=alpha)
        xs = _moe_call(u2_s, gates_s, x1_s, mods_s, lw, per_row=True, rows_per_batch=1, alpha=alpha)
        conv_new = jnp.concatenate([state_conv[li][:, 1:], xb_s[:, None, :]], axis=1)
        st_s.append((k_s.reshape(db, 1, h, hd), v_s.reshape(db, 1, h, hd), logf_s.reshape(db, 1, h), conv_new, h_new))

    stack = lambda sts, i: jnp.stack([s[i] for s in sts])
    return (xp.reshape(b, t_len, d), xs.reshape(db, 1, d),
            stack(st_p, 0), stack(st_p, 1), stack(st_p, 2), stack(st_p, 3), stack(st_p, 4),
            stack(st_s, 0), stack(st_s, 1), stack(st_s, 2), stack(st_s, 3), stack(st_s, 4))
```

```python
import functools
import math

import jax
import jax.numpy as jnp
from jax import lax
from jax.experimental import pallas as pl
from jax.experimental.pallas import tpu as pltpu

F32 = jnp.float32
BF16 = jnp.bfloat16

N_HEADS = 8
N_RNN_BLOCKS = 8
CONV_W = 4
LRU_C = 8.0
N_GROUPS = 4
EXPERTS_PER_GROUP = 8
N_EXPERTS = N_GROUPS * EXPERTS_PER_GROUP
GROUP_LANE = N_EXPERTS
LN_EPS = 1e-5
LANES = 128
SUBLANES = 8
MXU_N = 256
VMEM_LIMIT = 56 * 1024 * 1024

NEG_INF = float("-inf")
LOG2_E = math.log2(math.e)


def _cparams(sem):
    return pltpu.CompilerParams(dimension_semantics=sem, vmem_limit_bytes=VMEM_LIMIT)


def _split3(x):
    hi = x.astype(BF16)
    r1 = x - hi.astype(F32)
    mid = r1.astype(BF16)
    lo = (r1 - mid.astype(F32)).astype(BF16)
    return hi, mid, lo


def _dot(a, b):
    return jnp.dot(a, b, preferred_element_type=F32)


def _layer_norm(z, g, b):
    mu = jnp.mean(z, axis=-1, keepdims=True)
    zc = z - mu
    var = jnp.mean(zc * zc, axis=-1, keepdims=True)
    return zc * lax.rsqrt(var + LN_EPS) * g + b


def _log_sigmoid(x):
    return jnp.minimum(x, 0.0) - jnp.log1p(jnp.exp(-jnp.abs(x)))


def _softplus(x):
    return jnp.maximum(x, 0.0) + jnp.log1p(jnp.exp(-jnp.abs(x)))


def _gelu_tanh(x):
    c = math.sqrt(2.0 / math.pi)
    return 0.5 * x * (1.0 + jnp.tanh(c * (x + 0.044715 * (x * x * x))))


def _mod_spec(per_row, tm, d, chunk, tiles_per_batch):
    if per_row:
        return pl.BlockSpec((None, tm, d), lambda i: (0, i, chunk))
    return pl.BlockSpec((None, 1, d), lambda i: (i // tiles_per_batch, 0, chunk))


def _mods_kernel(c_ref, w_ref, b_ref, o_ref):
    c = c_ref[...]
    s = (c * jax.nn.sigmoid(c)).astype(BF16)
    o_ref[...] = _dot(s, w_ref[...].astype(BF16)) + b_ref[...]


def _mods_call(c_all, w_mod, b_mod):
    rc, d = c_all.shape
    n = w_mod.shape[1]
    tn = 1536
    return pl.pallas_call(
        _mods_kernel,
        grid=(n // tn,),
        in_specs=[pl.BlockSpec((rc, d), lambda j: (0, 0)),
                  pl.BlockSpec((d, tn), lambda j: (0, j)),
                  pl.BlockSpec((1, tn), lambda j: (0, j))],
        out_specs=pl.BlockSpec((rc, tn), lambda j: (0, j)),
        out_shape=jax.ShapeDtypeStruct((rc, n), F32),
        compiler_params=_cparams(("arbitrary",)),
        name="adaln_mods",
    )(c_all, w_mod, b_mod.reshape(1, n))


def _inproj_kernel(x_ref, shift_ref, scale_ref, w_ref, bf_ref, tri_ref,
                   xb_ref, gb_ref, q_ref, k_ref, v_ref, kb_ref, vb_ref, gl_ref, ga_ref, logf_ref, cum_ref,
                   carry_ref, *, tiles_per_batch, q_scale):
    i = pl.program_id(0)
    d = x_ref.shape[1]
    tm = x_ref.shape[0]
    u = (x_ref[...] * (1.0 + scale_ref[...]) + shift_ref[...]).astype(BF16)

    def mm(c):
        return _dot(u, w_ref[:, c * d:(c + 1) * d])

    xb_ref[...] = mm(0)
    gb_ref[...] = mm(1)
    q_ref[...] = (mm(2) * q_scale).astype(BF16)
    k = mm(3)
    k_ref[...] = k
    kb_ref[...] = k.astype(BF16)
    v = mm(4)
    v_ref[...] = v
    vb_ref[...] = v.astype(BF16)
    gl_ref[...] = mm(5)
    ga_ref[...] = mm(6)
    fg = _dot(u, w_ref[:, 7 * d:7 * d + LANES]) + bf_ref[...]
    logf = _log_sigmoid(fg)
    logf_ref[...] = logf[:, :N_HEADS]

    @pl.when(i % tiles_per_batch == 0)
    def _():
        carry_ref[...] = jnp.zeros_like(carry_ref)

    tri = tri_ref[...]
    hi, mid, lo = _split3(logf)
    cum = _dot(tri, hi) + _dot(tri, mid) + _dot(tri, lo) + carry_ref[...]
    cum_ref[...] = cum[:, :N_HEADS]
    carry_ref[...] = cum[tm - 1:tm, :]


def _inproj_call(x, mods, w_cat, bf_pad, *, per_row, rows_per_batch, q_scale):
    r, d = x.shape
    tm = min(256, r)
    tiles_per_batch = max(rows_per_batch // tm, 1)
    tri = (jnp.arange(tm)[:, None] >= jnp.arange(tm)[None, :]).astype(BF16)
    row_f32 = pl.BlockSpec((tm, d), lambda i: (i, 0))
    small = pl.BlockSpec((tm, N_HEADS), lambda i: (i, 0))
    sd = lambda dt: jax.ShapeDtypeStruct((r, d), dt)
    sh = jax.ShapeDtypeStruct((r, N_HEADS), F32)
    return pl.pallas_call(
        functools.partial(_inproj_kernel, tiles_per_batch=tiles_per_batch, q_scale=q_scale),
        grid=(r // tm,),
        in_specs=[row_f32,
                  _mod_spec(per_row, tm, d, 0, tiles_per_batch),
                  _mod_spec(per_row, tm, d, 1, tiles_per_batch),
                  pl.BlockSpec(w_cat.shape, lambda i: (0, 0), pipeline_mode=pl.Buffered(1)),
                  pl.BlockSpec((1, LANES), lambda i: (0, 0)),
                  pl.BlockSpec((tm, tm), lambda i: (0, 0))],
        out_specs=[row_f32, row_f32, row_f32, row_f32, row_f32, row_f32, row_f32, row_f32, row_f32, small, small],
        out_shape=[sd(F32), sd(F32), sd(BF16), sd(F32), sd(F32), sd(BF16), sd(BF16), sd(F32), sd(F32), sh, sh],
        scratch_shapes=[pltpu.VMEM((1, LANES), F32)],
        compiler_params=_cparams(("arbitrary",)),
        name="in_projection",
    )(x, mods, mods, w_cat, bf_pad, tri)


def _lru_gates(xc, wrg_ref, brg, wig_ref, big, lam):
    blk = xc.shape[1] // N_RNN_BLOCKS
    rs, is_ = [], []
    for n in range(N_RNN_BLOCKS):
        xn = xc[:, n * blk:(n + 1) * blk].astype(BF16)
        rs.append(_dot(xn, wrg_ref[n]))
        is_.append(_dot(xn, wig_ref[n]))
    r = jax.nn.sigmoid(jnp.concatenate(rs, axis=1) + brg)
    ig = jax.nn.sigmoid(jnp.concatenate(is_, axis=1) + big)
    log_a = (-LRU_C) * r * _softplus(-lam)
    a = jnp.exp(log_a)
    mult = jnp.sqrt(1.0 - jnp.exp(2.0 * log_a))
    return a, mult, ig


def _lru_prompt_kernel(xb_ref, gb_ref, cw_ref, cb_ref, wrg_ref, brg_ref, wig_ref, big_ref, lam_ref,
                       out_ref, hlast_ref, xp_ref, a_ref, b_ref, hs_ref, h_ref):
    t = pl.program_id(1)
    nt = pl.num_programs(1)
    tm = xb_ref.shape[0]
    halo = SUBLANES

    @pl.when(t == 0)
    def _():
        xp_ref[0:halo, :] = jnp.zeros((halo, xp_ref.shape[1]), F32)
        h_ref[...] = jnp.zeros_like(h_ref)

    xp_ref[halo:halo + tm, :] = xb_ref[...]
    xc = cb_ref[...]
    for tap in range(CONV_W):
        off = halo - (CONV_W - 1) + tap
        xc = xc + xp_ref[off:off + tm, :] * cw_ref[tap:tap + 1, :]
    a, mult, ig = _lru_gates(xc, wrg_ref, brg_ref[...], wig_ref, big_ref[...], lam_ref[...])
    row = lax.broadcasted_iota(jnp.int32, (tm, 1), 0)
    mult = jnp.where((row == 0) & (t == 0), 1.0, mult)
    a_ref[...] = a
    b_ref[...] = mult * ig * xc

    def step(s, h):
        h = a_ref[pl.ds(s, 1), :] * h + b_ref[pl.ds(s, 1), :]
        hs_ref[pl.ds(s, 1), :] = h
        return h

    h = lax.fori_loop(0, tm, step, h_ref[...], unroll=8)
    h_ref[...] = h
    xp_ref[halo - (CONV_W - 1):halo, :] = xp_ref[halo + tm - (CONV_W - 1):halo + tm, :]
    out_ref[...] = (hs_ref[...] * _gelu_tanh(gb_ref[...])).astype(BF16)

    @pl.when(t == nt - 1)
    def _():
        hlast_ref[...] = h


def _lru_prompt_call(xb, gb, b, t_len, lw):
    r, d = xb.shape
    tm = min(256, t_len)
    nt = t_len // tm
    row = pl.BlockSpec((tm, d), lambda bi, ti: (bi * nt + ti, 0))
    vec = pl.BlockSpec((1, d), lambda bi, ti: (0, 0))
    blk = d // N_RNN_BLOCKS
    wspec = pl.BlockSpec((N_RNN_BLOCKS, blk, blk), lambda bi, ti: (0, 0, 0))
    return pl.pallas_call(
        _lru_prompt_kernel,
        grid=(b, nt),
        in_specs=[row, row, pl.BlockSpec((CONV_W, d), lambda bi, ti: (0, 0)), vec, wspec, vec, wspec, vec, vec],
        out_specs=[row, pl.BlockSpec((None, 1, d), lambda bi, ti: (bi, 0, 0))],
        out_shape=[jax.ShapeDtypeStruct((r, d), BF16), jax.ShapeDtypeStruct((b, 1, d), F32)],
        scratch_shapes=[pltpu.VMEM((tm + SUBLANES, d), F32), pltpu.VMEM((tm, d), F32), pltpu.VMEM((tm, d), F32),
                        pltpu.VMEM((tm, d), F32), pltpu.VMEM((1, d), F32)],
        compiler_params=_cparams(("arbitrary", "arbitrary")),
        name="rglru_prompt",
    )(xb, gb, lw["conv_w"], lw["conv_b"], lw["w_rg"], lw["b_rg"], lw["w_ig"], lw["b_ig"], lw["lam"])


def _lru_sample_kernel(xb_ref, gb_ref, c0_ref, c1_ref, c2_ref, hprev_ref, cw_ref, cb_ref,
                       wrg_ref, brg_ref, wig_ref, big_ref, lam_ref, out_ref, hnew_ref):
    xb = xb_ref[...]
    xc = (cb_ref[...] + c0_ref[...] * cw_ref[0:1, :] + c1_ref[...] * cw_ref[1:2, :]
          + c2_ref[...] * cw_ref[2:3, :] + xb * cw_ref[3:4, :])
    a, mult, ig = _lru_gates(xc, wrg_ref, brg_ref[...], wig_ref, big_ref[...], lam_ref[...])
    h = a * hprev_ref[...] + mult * ig * xc
    hnew_ref[...] = h
    out_ref[...] = (h * _gelu_tanh(gb_ref[...])).astype(BF16)


def _lru_sample_call(xb, gb, conv_taps, h_prev, lw):
    r, d = xb.shape
    return pl.pallas_call(
        _lru_sample_kernel,
        out_shape=[jax.ShapeDtypeStruct((r, d), BF16), jax.ShapeDtypeStruct((r, d), F32)],
        compiler_params=pltpu.CompilerParams(vmem_limit_bytes=VMEM_LIMIT),
        name="rglru_sample",
    )(xb, gb, conv_taps[0], conv_taps[1], conv_taps[2], h_prev, lw["conv_w"], lw["conv_b"],
      lw["w_rg"], lw["b_rg"], lw["w_ig"], lw["b_ig"], lw["lam"])


def _attn_prompt_kernel(qi_ref, ki_ref, q_ref, k_ref, v_ref, fk_ref, o_ref, m_ref, acc_ref):
    p_id = pl.program_id(1)
    qi = qi_ref[p_id]
    ki = ki_ref[p_id]
    tq, tk = q_ref.shape[0], k_ref.shape[0]
    hd = q_ref.shape[1] // N_HEADS

    @pl.when(ki == 0)
    def _():
        m_ref[...] = jnp.full_like(m_ref, NEG_INF)
        acc_ref[...] = jnp.zeros_like(acc_ref)

    def update(masked):
        ones = jnp.ones((tk, hd), BF16)
        if masked:
            keep = (lax.broadcasted_iota(jnp.int32, (tq, tk), 0) >= lax.broadcasted_iota(jnp.int32, (tq, tk), 1))
        for h in range(N_HEADS):
            sl = slice(h * hd, (h + 1) * hd)
            s = lax.dot_general(q_ref[:, sl], k_ref[:, sl], (((1,), (1,)), ((), ())), preferred_element_type=F32)
            s = s - fk_ref[h:h + 1, :] * LOG2_E
            if masked:
                s = jnp.where(keep, s, NEG_INF)
            m_prev = m_ref[h]
            m_new = jnp.maximum(m_prev, jnp.max(s, axis=-1, keepdims=True))
            p = jnp.exp2(s - m_new).astype(BF16)
            pv = _dot(p, jnp.concatenate([v_ref[:, sl], ones], axis=1))
            acc_ref[h] = jnp.exp2(m_prev - m_new) * acc_ref[h] + pv
            m_ref[h] = m_new

    @pl.when(ki < qi)
    def _():
        update(False)

    @pl.when(ki == qi)
    def _():
        update(True)
        for h in range(N_HEADS):
            a = acc_ref[h]
            o_ref[:, h * hd:(h + 1) * hd] = (a[:, :hd] / a[:, hd:]).astype(o_ref.dtype)


def _attn_prompt_call(q, kb, vb, cum, b, t_len):
    r, d = q.shape
    hd = d // N_HEADS
    tq = min(512, t_len)
    nq = t_len // tq
    pairs = [(a, c) for a in range(nq) for c in range(a + 1)]
    qi_tab = jnp.asarray([p[0] for p in pairs], jnp.int32)
    ki_tab = jnp.asarray([p[1] for p in pairs], jnp.int32)
    fk = jnp.transpose(cum.reshape(b, t_len, N_HEADS), (0, 2, 1))
    grid_spec = pltpu.PrefetchScalarGridSpec(
        num_scalar_prefetch=2,
        grid=(b, len(pairs)),
        in_specs=[pl.BlockSpec((tq, d), lambda bi, p, qt, kt: (bi * nq + qt[p], 0)),
                  pl.BlockSpec((tq, d), lambda bi, p, qt, kt: (bi * nq + kt[p], 0)),
                  pl.BlockSpec((tq, d), lambda bi, p, qt, kt: (bi * nq + kt[p], 0)),
                  pl.BlockSpec((None, N_HEADS, tq), lambda bi, p, qt, kt: (bi, 0, kt[p]))],
        out_specs=pl.BlockSpec((tq, d), lambda bi, p, qt, kt: (bi * nq + qt[p], 0)),
        scratch_shapes=[pltpu.VMEM((N_HEADS, tq, 1), F32), pltpu.VMEM((N_HEADS, tq, 2 * hd), F32)],
    )
    return pl.pallas_call(
        _attn_prompt_kernel,
        grid_spec=grid_spec,
        out_shape=jax.ShapeDtypeStruct((r, d), BF16),
        compiler_params=_cparams(("arbitrary", "arbitrary")),
        name="fox_attention_prompt",
    )(qi_tab, ki_tab, q, kb, vb, fk)


def _logf_pages_kernel(l_ref, u_ref, ones_ref, pe_ref, po_ref, o_ref):
    page = l_ref.shape[1]
    parts = _split3(l_ref[...])
    insuf = sum(_dot(p, u_ref[...]) for p in parts) * LOG2_E
    hi = insuf.astype(BF16)
    mid = (insuf - hi.astype(F32)).astype(BF16)
    o_ref[:, 0:2 * page] = _dot(hi, pe_ref[...]) + _dot(mid, po_ref[...])
    o_ref[:, 2 * page:3 * page] = sum(_dot(p, ones_ref[...]) for p in parts) * LOG2_E


def _logf_pages_call(cache_logf):
    n_pool, page, h = cache_logf.shape
    rows = n_pool * h
    lft = jnp.swapaxes(cache_logf, 1, 2).reshape(rows, page)
    kk = jnp.arange(page)
    later = (kk[:, None] > kk[None, :]).astype(BF16)
    ones = jnp.ones((page, page), BF16)
    pe = (2 * kk[:, None] == jnp.arange(2 * page)[None, :]).astype(BF16)
    po = (2 * kk[:, None] + 1 == jnp.arange(2 * page)[None, :]).astype(BF16)
    tr = 2048
    while rows % tr:
        tr //= 2
    const = lambda shape: pl.BlockSpec(shape, lambda i: (0, 0))
    out = pl.pallas_call(
        _logf_pages_kernel,
        grid=(rows // tr,),
        in_specs=[pl.BlockSpec((tr, page), lambda i: (i, 0)), const((page, page)), const((page, page)),
                  const((page, 2 * page)), const((page, 2 * page))],
        out_specs=pl.BlockSpec((tr, 3 * page), lambda i: (i, 0)),
        out_shape=jax.ShapeDtypeStruct((rows, 3 * page), F32),
        compiler_params=_cparams(("arbitrary",)),
        name="logf_page_sums",
    )(lft, later, ones, pe, po)
    return out.reshape(n_pool, h, 3 * page)


def _attn_sample_kernel(pt_ref, q_ref, knew_ref, vnew_ref, cq_ref, ones_ref, *refs, pages_per_step):
    pp = pages_per_step
    k_refs, v_refs, ct_refs = refs[0:pp], refs[pp:2 * pp], refs[2 * pp:3 * pp]
    o_ref = refs[3 * pp]
    m_ref, l_ref, acc_ref, psuf_ref, mstep_ref, s_ref, psufs_ref = refs[3 * pp + 1:]
    j = pl.program_id(1)
    nj = pl.num_programs(1)
    page, h, hd = k_refs[0].shape
    q = q_ref[...]

    def lane_sums(lhs):
        return _dot(lhs.astype(BF16), ones_ref[...])

    @pl.when(j == 0)
    def _():
        prod = jnp.concatenate([knew_ref[...] * q, jnp.zeros((h, hd), F32)], axis=1)
        prod = jnp.concatenate([prod, jnp.zeros_like(prod)], axis=0)
        m_ref[...] = lane_sums(prod)[:h]
        l_ref[...] = jnp.ones_like(l_ref)
        acc_ref[...] = vnew_ref[...]
        psuf_ref[...] = cq_ref[...]
        mstep_ref[...] = jnp.full_like(mstep_ref, NEG_INF)
        s_ref[...] = jnp.full_like(s_ref, NEG_INF)
        psufs_ref[...] = jnp.zeros_like(psufs_ref)

    m_old = m_ref[...]
    m_new = jnp.maximum(m_old, mstep_ref[...])
    corr = jnp.exp2(m_old - m_new)
    l_new = l_ref[...] * corr
    acc = acc_ref[...] * corr
    for i in range(pp):
        p = jnp.exp2(s_ref[i] - (m_new - psufs_ref[i])[None])
        l_new = l_new + jnp.sum(p, axis=0)
        acc = acc + jnp.sum(p * v_refs[i][...], axis=0)
    l_ref[...] = l_new
    acc_ref[...] = acc
    m_ref[...] = m_new

    half = page // 2
    key = lax.broadcasted_iota(jnp.int32, (half, h, hd), 0)
    lane = lax.broadcasted_iota(jnp.int32, (half, h, hd), 2)
    own = (lane >> 1) == key
    psuf = psuf_ref[...]
    mstep = jnp.full((h, hd), NEG_INF, F32)
    for i in range(pp):
        prod = k_refs[i][...] * q[None]
        ct = ct_refs[i][...]
        bias = jnp.concatenate([jnp.where(own, ct[None, :, :hd], 0.0),
                                jnp.where(own, ct[None, :, hd:2 * hd], 0.0)], axis=0)
        lhs = jnp.concatenate([prod, bias], axis=2).reshape(page * h, 2 * hd)
        s = lane_sums(lhs).reshape(page, h, hd)
        mstep = jnp.maximum(mstep, jnp.max(s, axis=0) + psuf)
        s_ref[i] = s
        psufs_ref[i] = psuf
        psuf = psuf + ct[:, 2 * hd:]
    mstep_ref[...] = mstep
    psuf_ref[...] = psuf

    @pl.when(j == nj - 1)
    def _():
        o_ref[...] = acc_ref[...] / l_ref[...]


def _attn_sample_call(q, k_new, v_new, logf_new, cache_k, cache_v, ct_pages, page_table):
    db, d = q.shape
    n_pool, page, h, hd = cache_k.shape
    assert page == hd, "the two bias terms of a page's keys are packed into 2*HEAD_DIM lanes"
    n_pages = page_table.shape[1]
    pp = 8
    while n_pages % pp:
        pp //= 2
    n_steps = n_pages // pp
    cq = jnp.broadcast_to((logf_new * LOG2_E)[:, :, None], (db, h, hd))
    ones = jnp.ones((2 * hd, hd), BF16)

    def key_page(i, tail):
        def index(bi, j, pt):
            step = jnp.minimum(j, n_steps - 1)
            return (pt[bi, n_pages - 1 - (step * pp + i)],) + tail
        return index

    def value_page(i):
        def index(bi, j, pt):
            step = jnp.maximum(j - 1, 0)
            return (pt[bi, n_pages - 1 - (step * pp + i)], 0, 0, 0)
        return index

    tile = pl.BlockSpec((None, h, hd), lambda bi, j, pt: (bi, 0, 0))
    in_specs = [tile, tile, tile, tile, pl.BlockSpec((2 * hd, hd), lambda bi, j, pt: (0, 0))]
    in_specs += [pl.BlockSpec((None, page, h, hd), key_page(i, (0, 0, 0))) for i in range(pp)]
    in_specs += [pl.BlockSpec((None, page, h, hd), value_page(i)) for i in range(pp)]
    in_specs += [pl.BlockSpec((None, h, 3 * hd), key_page(i, (0, 0))) for i in range(pp)]
    grid_spec = pltpu.PrefetchScalarGridSpec(
        num_scalar_prefetch=1,
        grid=(db, n_steps + 1),
        in_specs=in_specs,
        out_specs=tile,
        scratch_shapes=[pltpu.VMEM((h, hd), F32)] * 5 + [pltpu.VMEM((pp, page, h, hd), F32),
                                                         pltpu.VMEM((pp, h, hd), F32)],
    )
    out = pl.pallas_call(
        functools.partial(_attn_sample_kernel, pages_per_step=pp),
        grid_spec=grid_spec,
        out_shape=jax.ShapeDtypeStruct((db, h, hd), F32),
        compiler_params=_cparams(("arbitrary", "arbitrary")),
        name="fox_attention_sample",
    )(page_table, q.astype(F32).reshape(db, h, hd), k_new.reshape(db, h, hd), v_new.reshape(db, h, hd), cq, ones,
      *([cache_k] * pp), *([cache_v] * pp), *([ct_pages] * pp))
    return out.reshape(db, d)


def _router_gates(logits):
    lane = lax.broadcasted_iota(jnp.int32, logits.shape, 1)
    lane_f = lane.astype(F32)
    big = float(4 * LANES)
    is_grp = (lane >= N_EXPERTS) & (lane < N_EXPERTS + N_GROUPS)
    gl = jnp.where(is_grp, logits, NEG_INF)
    gmax = jnp.max(gl, axis=-1, keepdims=True)
    gidx = jnp.min(jnp.where(gl == gmax, lane_f - N_EXPERTS, big), axis=-1, keepdims=True)
    g_p = 1.0 / jnp.sum(jnp.exp(gl - gmax), axis=-1, keepdims=True)
    in_grp = (lane < N_EXPERTS) & ((lane // EXPERTS_PER_GROUP).astype(F32) == gidx)
    e1 = jnp.where(in_grp, logits, NEG_INF)
    m1 = jnp.max(e1, axis=-1, keepdims=True)
    i1 = jnp.min(jnp.where(e1 == m1, lane_f, big), axis=-1, keepdims=True)
    e2 = jnp.where(lane_f == i1, NEG_INF, e1)
    m2 = jnp.max(e2, axis=-1, keepdims=True)
    i2 = jnp.min(jnp.where(e2 == m2, lane_f, big), axis=-1, keepdims=True)
    dlt = jnp.exp(m2 - m1)
    w1 = g_p / (1.0 + dlt)
    w2 = g_p * dlt / (1.0 + dlt)
    gates = jnp.where(lane_f == i1, w1, 0.0) + jnp.where(lane_f == i2, w2, 0.0)
    return jnp.where(lane == GROUP_LANE, gidx, gates)


def _merge_kernel(x_ref, lru_ref, att_ref, gl_ref, ga_ref, gate1_ref, shift2_ref, scale2_ref,
                  wl_ref, wa_ref, wo_ref, g_ref, b_ref, wr_ref, br_ref,
                  x1_ref, u2_ref, gates_ref, *, alpha):
    y = (jax.nn.sigmoid(gl_ref[...]) * _dot(lru_ref[...], wl_ref[...])
         + jax.nn.sigmoid(ga_ref[...]) * _dot(att_ref[...], wa_ref[...]))
    mix = _dot(y.astype(BF16), wo_ref[...])
    x1 = _layer_norm(alpha * x_ref[...] + gate1_ref[...] * mix, g_ref[...], b_ref[...])
    x1_ref[...] = x1
    u2 = x1 * (1.0 + scale2_ref[...]) + shift2_ref[...]
    u_hi = u2.astype(BF16)
    u2_ref[...] = u_hi
    u_lo = (u2 - u_hi.astype(F32)).astype(BF16)
    both = _dot(u_hi, wr_ref[...])
    logits = both[:, :LANES] + both[:, LANES:] + _dot(u_lo, wr_ref[:, :LANES]) + br_ref[...]
    gates_ref[...] = _router_gates(logits)


def _merge_call(x, lru, att, g_lru, g_att, mods, lw, *, per_row, rows_per_batch, alpha):
    r, d = x.shape
    tm = min(256, r)
    tpb = max(rows_per_batch // tm, 1)
    row = pl.BlockSpec((tm, d), lambda i: (i, 0))
    wsq = pl.BlockSpec((d, d), lambda i: (0, 0))
    vec = pl.BlockSpec((1, d), lambda i: (0, 0))
    return pl.pallas_call(
        functools.partial(_merge_kernel, alpha=alpha),
        grid=(r // tm,),
        in_specs=[row, row, row, row, row,
                  _mod_spec(per_row, tm, d, 2, tpb), _mod_spec(per_row, tm, d, 3, tpb), _mod_spec(per_row, tm, d, 4, tpb),
                  wsq, wsq, wsq, vec, vec,
                  pl.BlockSpec((d, 2 * LANES), lambda i: (0, 0)), pl.BlockSpec((1, LANES), lambda i: (0, 0))],
        out_specs=[row, row, pl.BlockSpec((tm, LANES), lambda i: (i, 0))],
        out_shape=[jax.ShapeDtypeStruct((r, d), F32), jax.ShapeDtypeStruct((r, d), BF16),
                   jax.ShapeDtypeStruct((r, LANES), F32)],
        compiler_params=_cparams(("arbitrary",)),
        name="merge_ln_router",
    )(x, lru, att, g_lru, g_att, mods, mods, mods, lw["w_br_lru"], lw["w_br_attn"], lw["w_out"],
      lw["ln1_g"], lw["ln1_b"], lw["w_router"], lw["b_router"])


def _moe_kernel(u_ref, route_ref, wg_ref, wu_ref, wd_ref, x1_ref, gate2_ref, g_ref, b_ref, y_ref,
                xs_ref, gs_ref, ys_ref, pos_ref, off_ref, *, alpha, chunk):
    g = pl.program_id(1)
    ng = pl.num_programs(1)
    t, d = u_ref.shape
    n_chunks = t // chunk
    epg = wg_ref.shape[0]
    lane = lax.broadcasted_iota(jnp.int32, (chunk, LANES), 1)

    @pl.when(g == 0)
    def _():
        tri = jnp.where(lax.broadcasted_iota(jnp.int32, (chunk, chunk), 0)
                        >= lax.broadcasted_iota(jnp.int32, (chunk, chunk), 1), 1.0, 0.0).astype(BF16)
        carry = jnp.zeros((1, LANES), F32)
        seen = []
        for c in range(n_chunks):
            grp = route_ref[c * chunk:(c + 1) * chunk, GROUP_LANE:GROUP_LANE + 1]
            mine = lane.astype(F32) == grp
            cum = _dot(tri, jnp.where(mine, 1.0, 0.0).astype(BF16)) + carry
            carry = cum[chunk - 1:chunk, :]
            seen.append((mine, cum))
        counts = jnp.broadcast_to(carry, (SUBLANES, LANES))
        lane8 = lax.broadcasted_iota(jnp.int32, (SUBLANES, LANES), 1)
        off = jnp.zeros((SUBLANES, LANES), F32)
        for sft in range(1, ng):
            off = off + jnp.where(lane8 >= sft, pltpu.roll(counts, sft, axis=1), 0.0)
        for c, (mine, cum) in enumerate(seen):
            pos = jnp.sum(jnp.where(mine, off[0:1, :] + cum - 1.0, 0.0), axis=-1, keepdims=True)
            pos_ref[c * chunk:(c + 1) * chunk, :] = jnp.broadcast_to(pos, (chunk, LANES))
        for gg in range(ng):
            off_ref[gg] = jnp.sum(jnp.where(lane8[0:1] == gg, off[0:1], 0.0)).astype(jnp.int32)
        off_ref[ng] = t
        pos_row = pos_ref[...].T[0:1, :]
        u = u_ref[...]
        r_parts = _split3(route_ref[...])
        for c in range(n_chunks):
            rows = slice(c * chunk, (c + 1) * chunk)
            dst = (lax.broadcasted_iota(jnp.int32, (chunk, t), 0) + c * chunk).astype(F32)
            pm = jnp.where(dst == pos_row, 1.0, 0.0).astype(BF16)
            xs_ref[rows, :] = _dot(pm, u).astype(BF16)
            gs_ref[rows, :] = sum(_dot(pm, p) for p in r_parts)
        ys_ref[...] = jnp.zeros_like(ys_ref)

    c_lo = off_ref[g] // chunk
    c_hi = (off_ref[g + 1] + chunk - 1) // chunk

    def run_chunk(c, carry):
        r0 = pl.multiple_of(c * chunk, chunk)
        xc = xs_ref[pl.ds(r0, chunk), :]
        gc = gs_ref[pl.ds(r0, chunk), :]
        acc = jnp.zeros((chunk, d), F32)
        for e in range(epg):
            ge = jnp.sum(jnp.where(lane == g * epg + e, gc, 0.0), axis=-1, keepdims=True)
            hg = _dot(xc, wg_ref[e])
            hid = hg * jax.nn.sigmoid(hg) * _dot(xc, wu_ref[e]) * ge
            acc = acc + _dot(hid.astype(BF16), wd_ref[e])
        ys_ref[pl.ds(r0, chunk), :] += acc
        return carry

    lax.fori_loop(c_lo, c_hi, run_chunk, 0)

    @pl.when(g == ng - 1)
    def _():
        ys = ys_ref[...]
        y_hi = ys.astype(BF16)
        y_lo = (ys - y_hi.astype(F32)).astype(BF16)
        gate2 = gate2_ref[...]
        for c in range(n_chunks):
            rows = slice(c * chunk, (c + 1) * chunk)
            src = lax.broadcasted_iota(jnp.int32, (chunk, t), 1).astype(F32)
            pm = jnp.where(src == pos_ref[rows, 0:1], 1.0, 0.0).astype(BF16)
            ffn = _dot(pm, y_hi) + _dot(pm, y_lo)
            g2 = gate2 if gate2.shape[0] == 1 else gate2[rows, :]
            y_ref[rows, :] = _layer_norm(alpha * x1_ref[rows, :] + g2 * ffn, g_ref[...], b_ref[...])


def _moe_call(u2, route, x1, mods, lw, *, per_row, rows_per_batch, alpha):
    r, d = x1.shape
    ng, epg, _, f = lw["w_gate"].shape
    tm = min(1024, r)
    chunk = min(128, tm)
    tpb = max(rows_per_batch // tm, 1)
    once = lambda w: pl.BlockSpec((tm, w), lambda i, g: (i, 0), pipeline_mode=pl.Buffered(1))
    vec = pl.BlockSpec((1, d), lambda i, g: (0, 0))
    if per_row:
        gate2 = pl.BlockSpec((None, tm, d), lambda i, g: (0, i, 5))
    else:
        gate2 = pl.BlockSpec((None, 1, d), lambda i, g: (i // tpb, 0, 5))
    return pl.pallas_call(
        functools.partial(_moe_kernel, alpha=alpha, chunk=chunk),
        grid=(r // tm, ng),
        in_specs=[once(d), once(LANES),
                  pl.BlockSpec((None, epg, d, f), lambda i, g: (g, 0, 0, 0)),
                  pl.BlockSpec((None, epg, d, f), lambda i, g: (g, 0, 0, 0)),
                  pl.BlockSpec((None, epg, f, d), lambda i, g: (g, 0, 0, 0)),
                  once(d), gate2, vec, vec],
        out_specs=pl.BlockSpec((tm, d), lambda i, g: (i, 0)),
        out_shape=jax.ShapeDtypeStruct((r, d), F32),
        scratch_shapes=[pltpu.VMEM((tm, d), BF16), pltpu.VMEM((tm, LANES), F32), pltpu.VMEM((tm, d), F32),
                        pltpu.VMEM((tm, LANES), F32), pltpu.SMEM((ng + 1,), jnp.int32)],
        compiler_params=_cparams(("arbitrary", "arbitrary")),
        name="moe_ffn_ln",
    )(u2, route, lw["w_gate"], lw["w_up"], lw["w_down"], x1, mods, lw["ln2_g"], lw["ln2_b"])


def _prep_layer_weights(li, w_in, b_f, conv_w, conv_b, w_rg, b_rg, w_ig, b_ig, lru_lambda, w_br_lru, w_br_attn,
                        w_out, ln1_g, ln1_b, w_grp, b_grp, w_rt, b_rt, w_gate, w_up, w_down, ln2_g, ln2_b):
    d = w_in.shape[1]
    h = b_f.shape[1]
    wi = w_in[li]
    c_main = 5 * d
    w_f = jnp.pad(wi[:, c_main:c_main + h], ((0, 0), (0, LANES - h)))
    w_cat = jnp.concatenate([wi[:, :c_main], wi[:, c_main + h:], w_f], axis=1).astype(BF16)
    w_router = jnp.pad(jnp.concatenate([w_rt[li], w_grp[li]], axis=1), ((0, 0), (0, LANES - N_EXPERTS - N_GROUPS)))
    w_router_hi = w_router.astype(BF16)
    w_router = jnp.concatenate([w_router_hi, (w_router - w_router_hi.astype(F32)).astype(BF16)], axis=1)
    b_router = jnp.pad(jnp.concatenate([b_rt[li], b_grp[li]]), (0, LANES - N_EXPERTS - N_GROUPS)).reshape(1, LANES)
    row = lambda v: v.reshape(1, -1)
    by_group = lambda w: w.astype(BF16).reshape(N_GROUPS, EXPERTS_PER_GROUP, *w.shape[1:])
    return {
        "w_cat": w_cat, "bf_pad": jnp.pad(b_f[li], (0, LANES - h)).reshape(1, LANES),
        "conv_w": conv_w[li], "conv_b": row(conv_b[li]),
        "w_rg": w_rg[li].astype(BF16), "b_rg": row(b_rg[li]), "w_ig": w_ig[li].astype(BF16), "b_ig": row(b_ig[li]),
        "lam": row(lru_lambda[li]),
        "w_br_lru": w_br_lru[li].astype(BF16), "w_br_attn": w_br_attn[li].astype(BF16), "w_out": w_out[li].astype(BF16),
        "ln1_g": row(ln1_g[li]), "ln1_b": row(ln1_b[li]), "ln2_g": row(ln2_g[li]), "ln2_b": row(ln2_b[li]),
        "w_router": w_router, "b_router": b_router,
        "w_gate": by_group(w_gate[li]), "w_up": by_group(w_up[li]), "w_down": by_group(w_down[li]),
    }


def kernel(x_prompt, x_sample, cache_k, cache_v, cache_logf, state_conv, state_h, page_table, c_prompt, c_sample, w_mod, b_mod, w_in, b_f, conv_w, conv_b, w_rg, b_rg, w_ig, b_ig, lru_lambda, w_br_lru, w_br_attn, w_out, ln1_g, ln1_b, w_grp, b_grp, w_rt, b_rt, w_gate, w_up, w_down, ln2_g, ln2_b):
    b, t_len, d = x_prompt.shape
    db, s_len, _ = x_sample.shape
    depth = w_mod.shape[0]
    h = N_HEADS
    hd = d // h
    assert s_len == 1, "the sample group decodes one token per sequence"
    assert t_len >= CONV_W - 1
    alpha = (2.0 * depth) ** 0.25
    q_scale = hd ** -0.5 * LOG2_E

    xp = x_prompt.reshape(b * t_len, d)
    xs = x_sample.reshape(db, d)
    rc = -(-(b + db) // SUBLANES) * SUBLANES
    c_all = jnp.concatenate([c_prompt, c_sample, jnp.zeros((rc - b - db, d), F32)], axis=0)
    st_p, st_s = [], []
    for li in range(depth):
        lw = _prep_layer_weights(li, w_in, b_f, conv_w, conv_b, w_rg, b_rg, w_ig, b_ig, lru_lambda, w_br_lru,
                                 w_br_attn, w_out, ln1_g, ln1_b, w_grp, b_grp, w_rt, b_rt, w_gate, w_up, w_down,
                                 ln2_g, ln2_b)
        mods = _mods_call(c_all, w_mod[li], b_mod[li])
        mods_p = mods[:b].reshape(b, 1, 6 * d)
        mods_s = mods[b:b + db].reshape(1, db, 6 * d)

        xb, gb, q, k, v, kb, vb, g_lru, g_att, logf, cum = _inproj_call(
            xp, mods_p, lw["w_cat"], lw["bf_pad"], per_row=False, rows_per_batch=t_len, q_scale=q_scale)
        lru, h_last = _lru_prompt_call(xb, gb, b, t_len, lw)
        att = _attn_prompt_call(q, kb, vb, cum, b, t_len)
        x1, u2, gates = _merge_call(xp, lru, att, g_lru, g_att, mods_p, lw, per_row=False, rows_per_batch=t_len, alpha=alpha)
        xp = _moe_call(u2, gates, x1, mods_p, lw, per_row=False, rows_per_batch=t_len, alpha=alpha)
        st_p.append((k.reshape(b, t_len, h, hd), v.reshape(b, t_len, h, hd), logf.reshape(b, t_len, h),
                     xb.reshape(b, t_len, d)[:, t_len - (CONV_W - 1):], h_last.reshape(b, d)))

        xb_s, gb_s, q_s, k_s, v_s, _, _, gl_s, ga_s, logf_s, _ = _inproj_call(
            xs, mods_s, lw["w_cat"], lw["bf_pad"], per_row=True, rows_per_batch=1, q_scale=q_scale)
        taps = [state_conv[li][:, tap, :] for tap in range(CONV_W - 1)]
        lru_s, h_new = _lru_sample_call(xb_s, gb_s, taps, state_h[li], lw)
        ct_pages = _logf_pages_call(cache_logf[li])
        att_s = _attn_sample_call(q_s, k_s, v_s, logf_s, cache_k[li], cache_v[li], ct_pages, page_table)
        x1_s, u2_s, gates_s = _merge_call(xs, lru_s, att_s.astype(BF16), gl_s, ga_s, mods_s, lw,
                                          per_row=True, rows_per_batch=1, alpha=alpha)
        xs = _moe_call(u2_s, gates_s, x1_s, mods_s, lw, per_row=True, rows_per_batch=1, alpha=alpha)
        conv_new = jnp.concatenate([state_conv[li][:, 1:], xb_s[:, None, :]], axis=1)
        st_s.append((k_s.reshape(db, 1, h, hd), v_s.reshape(db, 1, h, hd), logf_s.reshape(db, 1, h), conv_new, h_new))

    stack = lambda sts, i: jnp.stack([s[i] for s in sts])
    return (xp.reshape(b, t_len, d), xs.reshape(db, 1, d),
            stack(st_p, 0), stack(st_p, 1), stack(st_p, 2), stack(st_p, 3), stack(st_p, 4),
            stack(st_s, 0), stack(st_s, 1), stack(st_s, 2), stack(st_s, 3), stack(st_s, 4))
```

```python
import functools
import math

import jax
import jax.numpy as jnp
from jax import lax
from jax.experimental import pallas as pl
from jax.experimental.pallas import tpu as pltpu

F32 = jnp.float32
BF16 = jnp.bfloat16

N_HEADS = 8
N_RNN_BLOCKS = 8
CONV_W = 4
LRU_C = 8.0
N_GROUPS = 4
EXPERTS_PER_GROUP = 8
N_EXPERTS = N_GROUPS * EXPERTS_PER_GROUP
GROUP_LANE = N_EXPERTS
LN_EPS = 1e-5
LANES = 128
SUBLANES = 8
MXU_N = 256
VMEM_LIMIT = 56 * 1024 * 1024

NEG_INF = float("-inf")
LOG2_E = math.log2(math.e)


def _cparams(sem):
    return pltpu.CompilerParams(dimension_semantics=sem, vmem_limit_bytes=VMEM_LIMIT)


def _split3(x):
    hi = x.astype(BF16)
    r1 = x - hi.astype(F32)
    mid = r1.astype(BF16)
    lo = (r1 - mid.astype(F32)).astype(BF16)
    return hi, mid, lo


def _dot(a, b):
    return jnp.dot(a, b, preferred_element_type=F32)


def _layer_norm(z, g, b):
    mu = jnp.mean(z, axis=-1, keepdims=True)
    zc = z - mu
    var = jnp.mean(zc * zc, axis=-1, keepdims=True)
    return zc * lax.rsqrt(var + LN_EPS) * g + b


def _log_sigmoid(x):
    return jnp.minimum(x, 0.0) - jnp.log1p(jnp.exp(-jnp.abs(x)))


def _softplus(x):
    return jnp.maximum(x, 0.0) + jnp.log1p(jnp.exp(-jnp.abs(x)))


def _gelu_tanh(x):
    c = math.sqrt(2.0 / math.pi)
    return 0.5 * x * (1.0 + jnp.tanh(c * (x + 0.044715 * (x * x * x))))


def _mod_spec(per_row, tm, d, chunk, tiles_per_batch):
    if per_row:
        return pl.BlockSpec((None, tm, d), lambda i: (0, i, chunk))
    return pl.BlockSpec((None, 1, d), lambda i: (i // tiles_per_batch, 0, chunk))


def _mods_kernel(c_ref, w_ref, b_ref, o_ref):
    c = c_ref[...]
    s = (c * jax.nn.sigmoid(c)).astype(BF16)
    o_ref[...] = _dot(s, w_ref[...].astype(BF16)) + b_ref[...]


def _mods_call(c_all, w_mod, b_mod):
    rc, d = c_all.shape
    n = w_mod.shape[1]
    tn = 1536
    return pl.pallas_call(
        _mods_kernel,
        grid=(n // tn,),
        in_specs=[pl.BlockSpec((rc, d), lambda j: (0, 0)),
                  pl.BlockSpec((d, tn), lambda j: (0, j)),
                  pl.BlockSpec((1, tn), lambda j: (0, j))],
        out_specs=pl.BlockSpec((rc, tn), lambda j: (0, j)),
        out_shape=jax.ShapeDtypeStruct((rc, n), F32),
        compiler_params=_cparams(("arbitrary",)),
        name="adaln_mods",
    )(c_all, w_mod, b_mod.reshape(1, n))


def _inproj_kernel(x_ref, shift_ref, scale_ref, w_ref, bf_ref, tri_ref, sel_ref,
                   xb_ref, gb_ref, qx_ref, k_ref, v_ref, kx_ref, vb_ref, gl_ref, ga_ref, logf_ref, cum_ref,
                   carry_ref, *, tiles_per_batch, q_scale):
    i = pl.program_id(0)
    d = x_ref.shape[1]
    tm = x_ref.shape[0]
    hd = d // N_HEADS
    u = (x_ref[...] * (1.0 + scale_ref[...]) + shift_ref[...]).astype(BF16)

    def mm(c):
        return _dot(u, w_ref[:, c * d:(c + 1) * d])

    def store_heads(ref, main, extra):
        for h in range(N_HEADS):
            ref[:, 2 * h * hd:(2 * h + 1) * hd] = main[:, h * hd:(h + 1) * hd]
            ref[:, (2 * h + 1) * hd:(2 * h + 2) * hd] = extra if extra.shape[1] == hd else extra[:, h * hd:(h + 1) * hd]

    xb_ref[...] = mm(0)
    gb_ref[...] = mm(1)
    three_ones = jnp.where(lax.broadcasted_iota(jnp.int32, (tm, hd), 1) < 3, 1.0, 0.0).astype(BF16)
    store_heads(qx_ref, (mm(2) * q_scale).astype(BF16), three_ones)
    k = mm(3)
    k_ref[...] = k
    v = mm(4)
    v_ref[...] = v
    vb_ref[...] = v.astype(BF16)
    gl_ref[...] = mm(5)
    ga_ref[...] = mm(6)
    fg = _dot(u, w_ref[:, 7 * d:7 * d + LANES]) + bf_ref[...]
    logf = _log_sigmoid(fg)
    logf_ref[...] = logf[:, :N_HEADS]

    @pl.when(i % tiles_per_batch == 0)
    def _():
        carry_ref[...] = jnp.zeros_like(carry_ref)

    tri = tri_ref[...]
    hi, mid, lo = _split3(logf)
    cum = _dot(tri, hi) + _dot(tri, mid) + _dot(tri, lo) + carry_ref[...]
    cum_ref[...] = cum[:, :N_HEADS]
    carry_ref[...] = cum[tm - 1:tm, :]
    terms = jnp.concatenate(_split3(cum * (-LOG2_E)), axis=1)
    store_heads(kx_ref, k.astype(BF16), _dot(terms, sel_ref[...]).astype(BF16))


def _inproj_call(x, mods, w_cat, bf_pad, *, per_row, rows_per_batch, q_scale):
    r, d = x.shape
    hd = d // N_HEADS
    tm = min(256, r)
    tiles_per_batch = max(rows_per_batch // tm, 1)
    tri = (jnp.arange(tm)[:, None] >= jnp.arange(tm)[None, :]).astype(BF16)
    src = jnp.arange(3 * LANES)
    dst = jnp.arange(d)
    sel = ((src[:, None] % LANES == dst[None, :] // hd) & (src[:, None] // LANES == dst[None, :] % hd)).astype(BF16)
    row_f32 = pl.BlockSpec((tm, d), lambda i: (i, 0))
    wide = pl.BlockSpec((tm, 2 * d), lambda i: (i, 0))
    small = pl.BlockSpec((tm, N_HEADS), lambda i: (i, 0))
    sd = lambda dt: jax.ShapeDtypeStruct((r, d), dt)
    sw = jax.ShapeDtypeStruct((r, 2 * d), BF16)
    sh = jax.ShapeDtypeStruct((r, N_HEADS), F32)
    return pl.pallas_call(
        functools.partial(_inproj_kernel, tiles_per_batch=tiles_per_batch, q_scale=q_scale),
        grid=(r // tm,),
        in_specs=[row_f32,
                  _mod_spec(per_row, tm, d, 0, tiles_per_batch),
                  _mod_spec(per_row, tm, d, 1, tiles_per_batch),
                  pl.BlockSpec(w_cat.shape, lambda i: (0, 0), pipeline_mode=pl.Buffered(1)),
                  pl.BlockSpec((1, LANES), lambda i: (0, 0)),
                  pl.BlockSpec((tm, tm), lambda i: (0, 0)),
                  pl.BlockSpec((3 * LANES, d), lambda i: (0, 0))],
        out_specs=[row_f32, row_f32, wide, row_f32, row_f32, wide, row_f32, row_f32, row_f32, small, small],
        out_shape=[sd(F32), sd(F32), sw, sd(F32), sd(F32), sw, sd(BF16), sd(F32), sd(F32), sh, sh],
        scratch_shapes=[pltpu.VMEM((1, LANES), F32)],
        compiler_params=_cparams(("arbitrary",)),
        name="in_projection",
    )(x, mods, mods, w_cat, bf_pad, tri, sel)


def _lru_gates(xc, wrg_ref, brg, wig_ref, big, lam):
    blk = xc.shape[1] // N_RNN_BLOCKS
    rs, is_ = [], []
    for n in range(N_RNN_BLOCKS):
        xn = xc[:, n * blk:(n + 1) * blk].astype(BF16)
        rs.append(_dot(xn, wrg_ref[n]))
        is_.append(_dot(xn, wig_ref[n]))
    r = jax.nn.sigmoid(jnp.concatenate(rs, axis=1) + brg)
    ig = jax.nn.sigmoid(jnp.concatenate(is_, axis=1) + big)
    log_a = (-LRU_C) * r * _softplus(-lam)
    a = jnp.exp(log_a)
    mult = jnp.sqrt(1.0 - jnp.exp(2.0 * log_a))
    return a, mult, ig


def _lru_prompt_kernel(xb_ref, gb_ref, cw_ref, cb_ref, wrg_ref, brg_ref, wig_ref, big_ref, lam_ref,
                       out_ref, hlast_ref, xp_ref, a_ref, b_ref, hs_ref, h_ref):
    t = pl.program_id(1)
    nt = pl.num_programs(1)
    tm = xb_ref.shape[0]
    halo = SUBLANES

    @pl.when(t == 0)
    def _():
        xp_ref[0:halo, :] = jnp.zeros((halo, xp_ref.shape[1]), F32)
        h_ref[...] = jnp.zeros_like(h_ref)

    xp_ref[halo:halo + tm, :] = xb_ref[...]
    xc = cb_ref[...]
    for tap in range(CONV_W):
        off = halo - (CONV_W - 1) + tap
        xc = xc + xp_ref[off:off + tm, :] * cw_ref[tap:tap + 1, :]
    a, mult, ig = _lru_gates(xc, wrg_ref, brg_ref[...], wig_ref, big_ref[...], lam_ref[...])
    row = lax.broadcasted_iota(jnp.int32, (tm, 1), 0)
    mult = jnp.where((row == 0) & (t == 0), 1.0, mult)
    a_ref[...] = a
    b_ref[...] = mult * ig * xc

    def step(s, h):
        h = a_ref[pl.ds(s, 1), :] * h + b_ref[pl.ds(s, 1), :]
        hs_ref[pl.ds(s, 1), :] = h
        return h

    h = lax.fori_loop(0, tm, step, h_ref[...], unroll=8)
    h_ref[...] = h
    xp_ref[halo - (CONV_W - 1):halo, :] = xp_ref[halo + tm - (CONV_W - 1):halo + tm, :]
    out_ref[...] = (hs_ref[...] * _gelu_tanh(gb_ref[...])).astype(BF16)

    @pl.when(t == nt - 1)
    def _():
        hlast_ref[...] = h


def _lru_prompt_call(xb, gb, b, t_len, lw):
    r, d = xb.shape
    tm = min(256, t_len)
    nt = t_len // tm
    row = pl.BlockSpec((tm, d), lambda bi, ti: (bi * nt + ti, 0))
    vec = pl.BlockSpec((1, d), lambda bi, ti: (0, 0))
    blk = d // N_RNN_BLOCKS
    wspec = pl.BlockSpec((N_RNN_BLOCKS, blk, blk), lambda bi, ti: (0, 0, 0))
    return pl.pallas_call(
        _lru_prompt_kernel,
        grid=(b, nt),
        in_specs=[row, row, pl.BlockSpec((CONV_W, d), lambda bi, ti: (0, 0)), vec, wspec, vec, wspec, vec, vec],
        out_specs=[row, pl.BlockSpec((None, 1, d), lambda bi, ti: (bi, 0, 0))],
        out_shape=[jax.ShapeDtypeStruct((r, d), BF16), jax.ShapeDtypeStruct((b, 1, d), F32)],
        scratch_shapes=[pltpu.VMEM((tm + SUBLANES, d), F32), pltpu.VMEM((tm, d), F32), pltpu.VMEM((tm, d), F32),
                        pltpu.VMEM((tm, d), F32), pltpu.VMEM((1, d), F32)],
        compiler_params=_cparams(("arbitrary", "arbitrary")),
        name="rglru_prompt",
    )(xb, gb, lw["conv_w"], lw["conv_b"], lw["w_rg"], lw["b_rg"], lw["w_ig"], lw["b_ig"], lw["lam"])


def _lru_sample_kernel(xb_ref, gb_ref, c0_ref, c1_ref, c2_ref, hprev_ref, cw_ref, cb_ref,
                       wrg_ref, brg_ref, wig_ref, big_ref, lam_ref, out_ref, hnew_ref):
    xb = xb_ref[...]
    xc = (cb_ref[...] + c0_ref[...] * cw_ref[0:1, :] + c1_ref[...] * cw_ref[1:2, :]
          + c2_ref[...] * cw_ref[2:3, :] + xb * cw_ref[3:4, :])
    a, mult, ig = _lru_gates(xc, wrg_ref, brg_ref[...], wig_ref, big_ref[...], lam_ref[...])
    h = a * hprev_ref[...] + mult * ig * xc
    hnew_ref[...] = h
    out_ref[...] = (h * _gelu_tanh(gb_ref[...])).astype(BF16)


def _lru_sample_call(xb, gb, conv_taps, h_prev, lw):
    r, d = xb.shape
    return pl.pallas_call(
        _lru_sample_kernel,
        out_shape=[jax.ShapeDtypeStruct((r, d), BF16), jax.ShapeDtypeStruct((r, d), F32)],
        compiler_params=pltpu.CompilerParams(vmem_limit_bytes=VMEM_LIMIT),
        name="rglru_sample",
    )(xb, gb, conv_taps[0], conv_taps[1], conv_taps[2], h_prev, lw["conv_w"], lw["conv_b"],
      lw["w_rg"], lw["b_rg"], lw["w_ig"], lw["b_ig"], lw["lam"])


def _attn_prompt_kernel(qi_ref, ki_ref, q_ref, k_ref, v_ref, o_ref, m_ref, acc_ref):
    p_id = pl.program_id(1)
    qi = qi_ref[p_id]
    ki = ki_ref[p_id]
    tq, tk = q_ref.shape[0], k_ref.shape[0]
    hd = v_ref.shape[1] // N_HEADS

    @pl.when(ki == 0)
    def _():
        m_ref[...] = jnp.full_like(m_ref, NEG_INF)
        acc_ref[...] = jnp.zeros_like(acc_ref)

    def update(masked):
        ones = jnp.ones((tk, hd), BF16)
        if masked:
            keep = (lax.broadcasted_iota(jnp.int32, (tq, tk), 0) >= lax.broadcasted_iota(jnp.int32, (tq, tk), 1))
        for h in range(N_HEADS):
            sl = slice(h * hd, (h + 1) * hd)
            sl2 = slice(2 * h * hd, 2 * (h + 1) * hd)
            s = lax.dot_general(q_ref[:, sl2], k_ref[:, sl2], (((1,), (1,)), ((), ())), preferred_element_type=F32)
            if masked:
                s = jnp.where(keep, s, NEG_INF)
            m_prev = m_ref[h]
            m_new = jnp.maximum(m_prev, jnp.max(s, axis=-1, keepdims=True))
            p = jnp.exp2(s - m_new).astype(BF16)
            pv = _dot(p, jnp.concatenate([v_ref[:, sl], ones], axis=1))
            acc_ref[h] = jnp.exp2(m_prev - m_new) * acc_ref[h] + pv
            m_ref[h] = m_new

    @pl.when(ki < qi)
    def _():
        update(False)

    @pl.when(ki == qi)
    def _():
        update(True)
        for h in range(N_HEADS):
            a = acc_ref[h]
            o_ref[:, h * hd:(h + 1) * hd] = (a[:, :hd] / a[:, hd:]).astype(o_ref.dtype)


def _attn_prompt_call(qx, kx, vb, b, t_len):
    r, d = vb.shape
    hd = d // N_HEADS
    tq = min(512, t_len)
    nq = t_len // tq
    pairs = [(a, c) for a in range(nq) for c in range(a + 1)]
    qi_tab = jnp.asarray([p[0] for p in pairs], jnp.int32)
    ki_tab = jnp.asarray([p[1] for p in pairs], jnp.int32)
    grid_spec = pltpu.PrefetchScalarGridSpec(
        num_scalar_prefetch=2,
        grid=(b, len(pairs)),
        in_specs=[pl.BlockSpec((tq, 2 * d), lambda bi, p, qt, kt: (bi * nq + qt[p], 0)),
                  pl.BlockSpec((tq, 2 * d), lambda bi, p, qt, kt: (bi * nq + kt[p], 0)),
                  pl.BlockSpec((tq, d), lambda bi, p, qt, kt: (bi * nq + kt[p], 0))],
        out_specs=pl.BlockSpec((tq, d), lambda bi, p, qt, kt: (bi * nq + qt[p], 0)),
        scratch_shapes=[pltpu.VMEM((N_HEADS, tq, 1), F32), pltpu.VMEM((N_HEADS, tq, 2 * hd), F32)],
    )
    return pl.pallas_call(
        _attn_prompt_kernel,
        grid_spec=grid_spec,
        out_shape=jax.ShapeDtypeStruct((r, d), BF16),
        compiler_params=_cparams(("arbitrary", "arbitrary")),
        name="fox_attention_prompt",
    )(qi_tab, ki_tab, qx, kx, vb)


def _logf_pages_kernel(l_ref, u_ref, ones_ref, pe_ref, po_ref, o_ref):
    page = l_ref.shape[1]
    parts = _split3(l_ref[...])
    insuf = sum(_dot(p, u_ref[...]) for p in parts) * LOG2_E
    hi = insuf.astype(BF16)
    mid = (insuf - hi.astype(F32)).astype(BF16)
    o_ref[:, 0:2 * page] = _dot(hi, pe_ref[...]) + _dot(mid, po_ref[...])
    o_ref[:, 2 * page:3 * page] = sum(_dot(p, ones_ref[...]) for p in parts) * LOG2_E


def _logf_pages_call(cache_logf):
    n_pool, page, h = cache_logf.shape
    rows = n_pool * h
    lft = jnp.swapaxes(cache_logf, 1, 2).reshape(rows, page)
    kk = jnp.arange(page)
    later = (kk[:, None] > kk[None, :]).astype(BF16)
    ones = jnp.ones((page, page), BF16)
    pe = (2 * kk[:, None] == jnp.arange(2 * page)[None, :]).astype(BF16)
    po = (2 * kk[:, None] + 1 == jnp.arange(2 * page)[None, :]).astype(BF16)
    tr = 2048
    while rows % tr:
        tr //= 2
    const = lambda shape: pl.BlockSpec(shape, lambda i: (0, 0))
    out = pl.pallas_call(
        _logf_pages_kernel,
        grid=(rows // tr,),
        in_specs=[pl.BlockSpec((tr, page), lambda i: (i, 0)), const((page, page)), const((page, page)),
                  const((page, 2 * page)), const((page, 2 * page))],
        out_specs=pl.BlockSpec((tr, 3 * page), lambda i: (i, 0)),
        out_shape=jax.ShapeDtypeStruct((rows, 3 * page), F32),
        compiler_params=_cparams(("arbitrary",)),
        name="logf_page_sums",
    )(lft, later, ones, pe, po)
    return out.reshape(n_pool, h, 3 * page)


def _attn_sample_kernel(pt_ref, q_ref, knew_ref, vnew_ref, cq_ref, ones_ref, *refs, pages_per_step):
    pp = pages_per_step
    k_refs, v_refs, ct_refs = refs[0:pp], refs[pp:2 * pp], refs[2 * pp:3 * pp]
    o_ref = refs[3 * pp]
    m_ref, l_ref, acc_ref, psuf_ref, mstep_ref, s_ref, psufs_ref = refs[3 * pp + 1:]
    j = pl.program_id(1)
    nj = pl.num_programs(1)
    page, h, hd = k_refs[0].shape
    q = q_ref[...]

    def lane_sums(lhs):
        return _dot(lhs.astype(BF16), ones_ref[...])

    @pl.when(j == 0)
    def _():
        prod = jnp.concatenate([knew_ref[...] * q, jnp.zeros((h, hd), F32)], axis=1)
        prod = jnp.concatenate([prod, jnp.zeros_like(prod)], axis=0)
        m_ref[...] = lane_sums(prod)[:h]
        l_ref[...] = jnp.ones_like(l_ref)
        acc_ref[...] = vnew_ref[...]
        psuf_ref[...] = cq_ref[...]
        mstep_ref[...] = jnp.full_like(mstep_ref, NEG_INF)
        s_ref[...] = jnp.full_like(s_ref, NEG_INF)
        psufs_ref[...] = jnp.zeros_like(psufs_ref)

    m_old = m_ref[...]
    m_new = jnp.maximum(m_old, mstep_ref[...])
    corr = jnp.exp2(m_old - m_new)
    l_new = l_ref[...] * corr
    acc = acc_ref[...] * corr
    for i in range(pp):
        p = jnp.exp2(s_ref[i] - (m_new - psufs_ref[i])[None])
        l_new = l_new + jnp.sum(p, axis=0)
        acc = acc + jnp.sum(p * v_refs[i][...], axis=0)
    l_ref[...] = l_new
    acc_ref[...] = acc
    m_ref[...] = m_new

    half = page // 2
    key = lax.broadcasted_iota(jnp.int32, (half, h, hd), 0)
    lane = lax.broadcasted_iota(jnp.int32, (half, h, hd), 2)
    own = (lane >> 1) == key
    psuf = psuf_ref[...]
    mstep = jnp.full((h, hd), NEG_INF, F32)
    for i in range(pp):
        prod = k_refs[i][...] * q[None]
        ct = ct_refs[i][...]
        bias = jnp.concatenate([jnp.where(own, ct[None, :, :hd], 0.0),
                                jnp.where(own, ct[None, :, hd:2 * hd], 0.0)], axis=0)
        lhs = jnp.concatenate([prod, bias], axis=2).reshape(page * h, 2 * hd)
        s = lane_sums(lhs).reshape(page, h, hd)
        mstep = jnp.maximum(mstep, jnp.max(s, axis=0) + psuf)
        s_ref[i] = s
        psufs_ref[i] = psuf
        psuf = psuf + ct[:, 2 * hd:]
    mstep_ref[...] = mstep
    psuf_ref[...] = psuf

    @pl.when(j == nj - 1)
    def _():
        o_ref[...] = acc_ref[...] / l_ref[...]


def _attn_sample_call(q, k_new, v_new, logf_new, cache_k, cache_v, ct_pages, page_table):
    db, d = q.shape
    n_pool, page, h, hd = cache_k.shape
    assert page == hd, "the two bias terms of a page's keys are packed into 2*HEAD_DIM lanes"
    n_pages = page_table.shape[1]
    pp = 8
    while n_pages % pp:
        pp //= 2
    n_steps = n_pages // pp
    cq = jnp.broadcast_to((logf_new * LOG2_E)[:, :, None], (db, h, hd))
    ones = jnp.ones((2 * hd, hd), BF16)

    def key_page(i, tail):
        def index(bi, j, pt):
            step = jnp.minimum(j, n_steps - 1)
            return (pt[bi, n_pages - 1 - (step * pp + i)],) + tail
        return index

    def value_page(i):
        def index(bi, j, pt):
            step = jnp.maximum(j - 1, 0)
            return (pt[bi, n_pages - 1 - (step * pp + i)], 0, 0, 0)
        return index

    tile = pl.BlockSpec((None, h, hd), lambda bi, j, pt: (bi, 0, 0))
    in_specs = [tile, tile, tile, tile, pl.BlockSpec((2 * hd, hd), lambda bi, j, pt: (0, 0))]
    in_specs += [pl.BlockSpec((None, page, h, hd), key_page(i, (0, 0, 0))) for i in range(pp)]
    in_specs += [pl.BlockSpec((None, page, h, hd), value_page(i)) for i in range(pp)]
    in_specs += [pl.BlockSpec((None, h, 3 * hd), key_page(i, (0, 0))) for i in range(pp)]
    grid_spec = pltpu.PrefetchScalarGridSpec(
        num_scalar_prefetch=1,
        grid=(db, n_steps + 1),
        in_specs=in_specs,
        out_specs=tile,
        scratch_shapes=[pltpu.VMEM((h, hd), F32)] * 5 + [pltpu.VMEM((pp, page, h, hd), F32),
                                                         pltpu.VMEM((pp, h, hd), F32)],
    )
    out = pl.pallas_call(
        functools.partial(_attn_sample_kernel, pages_per_step=pp),
        grid_spec=grid_spec,
        out_shape=jax.ShapeDtypeStruct((db, h, hd), F32),
        compiler_params=_cparams(("arbitrary", "arbitrary")),
        name="fox_attention_sample",
    )(page_table, q.astype(F32).reshape(db, h, hd), k_new.reshape(db, h, hd), v_new.reshape(db, h, hd), cq, ones,
      *([cache_k] * pp), *([cache_v] * pp), *([ct_pages] * pp))
    return out.reshape(db, d)


def _router_gates(logits):
    lane = lax.broadcasted_iota(jnp.int32, logits.shape, 1)
    lane_f = lane.astype(F32)
    big = float(4 * LANES)
    is_grp = (lane >= N_EXPERTS) & (lane < N_EXPERTS + N_GROUPS)
    gl = jnp.where(is_grp, logits, NEG_INF)
    gmax = jnp.max(gl, axis=-1, keepdims=True)
    gidx = jnp.min(jnp.where(gl == gmax, lane_f - N_EXPERTS, big), axis=-1, keepdims=True)
    g_p = 1.0 / jnp.sum(jnp.exp(gl - gmax), axis=-1, keepdims=True)
    in_grp = (lane < N_EXPERTS) & ((lane // EXPERTS_PER_GROUP).astype(F32) == gidx)
    e1 = jnp.where(in_grp, logits, NEG_INF)
    m1 = jnp.max(e1, axis=-1, keepdims=True)
    i1 = jnp.min(jnp.where(e1 == m1, lane_f, big), axis=-1, keepdims=True)
    e2 = jnp.where(lane_f == i1, NEG_INF, e1)
    m2 = jnp.max(e2, axis=-1, keepdims=True)
    i2 = jnp.min(jnp.where(e2 == m2, lane_f, big), axis=-1, keepdims=True)
    dlt = jnp.exp(m2 - m1)
    w1 = g_p / (1.0 + dlt)
    w2 = g_p * dlt / (1.0 + dlt)
    gates = jnp.where(lane_f == i1, w1, 0.0) + jnp.where(lane_f == i2, w2, 0.0)
    return jnp.where(lane == GROUP_LANE, gidx, gates)


def _merge_kernel(x_ref, lru_ref, att_ref, gl_ref, ga_ref, gate1_ref, shift2_ref, scale2_ref,
                  wl_ref, wa_ref, wo_ref, g_ref, b_ref, wr_ref, br_ref,
                  x1_ref, u2_ref, gates_ref, *, alpha):
    y = (jax.nn.sigmoid(gl_ref[...]) * _dot(lru_ref[...], wl_ref[...])
         + jax.nn.sigmoid(ga_ref[...]) * _dot(att_ref[...], wa_ref[...]))
    mix = _dot(y.astype(BF16), wo_ref[...])
    x1 = _layer_norm(alpha * x_ref[...] + gate1_ref[...] * mix, g_ref[...], b_ref[...])
    x1_ref[...] = x1
    u2 = x1 * (1.0 + scale2_ref[...]) + shift2_ref[...]
    u_hi = u2.astype(BF16)
    u2_ref[...] = u_hi
    u_lo = (u2 - u_hi.astype(F32)).astype(BF16)
    both = _dot(u_hi, wr_ref[...])
    logits = both[:, :LANES] + both[:, LANES:] + _dot(u_lo, wr_ref[:, :LANES]) + br_ref[...]
    gates_ref[...] = _router_gates(logits)


def _merge_call(x, lru, att, g_lru, g_att, mods, lw, *, per_row, rows_per_batch, alpha):
    r, d = x.shape
    tm = min(512, r)
    tpb = max(rows_per_batch // tm, 1)
    row = pl.BlockSpec((tm, d), lambda i: (i, 0))
    wsq = pl.BlockSpec((d, d), lambda i: (0, 0))
    vec = pl.BlockSpec((1, d), lambda i: (0, 0))
    return pl.pallas_call(
        functools.partial(_merge_kernel, alpha=alpha),
        grid=(r // tm,),
        in_specs=[row, row, row, row, row,
                  _mod_spec(per_row, tm, d, 2, tpb), _mod_spec(per_row, tm, d, 3, tpb), _mod_spec(per_row, tm, d, 4, tpb),
                  wsq, wsq, wsq, vec, vec,
                  pl.BlockSpec((d, 2 * LANES), lambda i: (0, 0)), pl.BlockSpec((1, LANES), lambda i: (0, 0))],
        out_specs=[row, row, pl.BlockSpec((tm, LANES), lambda i: (i, 0))],
        out_shape=[jax.ShapeDtypeStruct((r, d), F32), jax.ShapeDtypeStruct((r, d), BF16),
                   jax.ShapeDtypeStruct((r, LANES), F32)],
        compiler_params=_cparams(("arbitrary",)),
        name="merge_ln_router",
    )(x, lru, att, g_lru, g_att, mods, mods, mods, lw["w_br_lru"], lw["w_br_attn"], lw["w_out"],
      lw["ln1_g"], lw["ln1_b"], lw["w_router"], lw["b_router"])


def _moe_kernel(u_ref, route_ref, wg_ref, wu_ref, wd_ref, x1_ref, gate2_ref, g_ref, b_ref, y_ref,
                xs_ref, gs_ref, ys_ref, pos_ref, off_ref, *, alpha, chunk):
    g = pl.program_id(1)
    ng = pl.num_programs(1)
    t, d = u_ref.shape
    n_chunks = t // chunk
    epg = wg_ref.shape[0]
    lane = lax.broadcasted_iota(jnp.int32, (chunk, LANES), 1)

    @pl.when(g == 0)
    def _():
        tri = jnp.where(lax.broadcasted_iota(jnp.int32, (chunk, chunk), 0)
                        >= lax.broadcasted_iota(jnp.int32, (chunk, chunk), 1), 1.0, 0.0).astype(BF16)
        carry = jnp.zeros((1, LANES), F32)
        seen = []
        for c in range(n_chunks):
            grp = route_ref[c * chunk:(c + 1) * chunk, GROUP_LANE:GROUP_LANE + 1]
            mine = lane.astype(F32) == grp
            cum = _dot(tri, jnp.where(mine, 1.0, 0.0).astype(BF16)) + carry
            carry = cum[chunk - 1:chunk, :]
            seen.append((mine, cum))
        counts = jnp.broadcast_to(carry, (SUBLANES, LANES))
        lane8 = lax.broadcasted_iota(jnp.int32, (SUBLANES, LANES), 1)
        off = jnp.zeros((SUBLANES, LANES), F32)
        for sft in range(1, ng):
            off = off + jnp.where(lane8 >= sft, pltpu.roll(counts, sft, axis=1), 0.0)
        for c, (mine, cum) in enumerate(seen):
            pos = jnp.sum(jnp.where(mine, off[0:1, :] + cum - 1.0, 0.0), axis=-1, keepdims=True)
            pos_ref[c * chunk:(c + 1) * chunk, :] = jnp.broadcast_to(pos, (chunk, LANES))
        for gg in range(ng):
            off_ref[gg] = jnp.sum(jnp.where(lane8[0:1] == gg, off[0:1], 0.0)).astype(jnp.int32)
        off_ref[ng] = t
        pos_row = pos_ref[...].T[0:1, :]
        u = u_ref[...]
        r_parts = _split3(route_ref[...])
        for c in range(n_chunks):
            rows = slice(c * chunk, (c + 1) * chunk)
            dst = (lax.broadcasted_iota(jnp.int32, (chunk, t), 0) + c * chunk).astype(F32)
            pm = jnp.where(dst == pos_row, 1.0, 0.0).astype(BF16)
            xs_ref[rows, :] = _dot(pm, u).astype(BF16)
            gs_ref[rows, :] = sum(_dot(pm, p) for p in r_parts)
        ys_ref[...] = jnp.zeros_like(ys_ref)

    c_lo = off_ref[g] // chunk
    c_hi = (off_ref[g + 1] + chunk - 1) // chunk

    def run_chunk(c, carry):
        r0 = pl.multiple_of(c * chunk, chunk)
        xc = xs_ref[pl.ds(r0, chunk), :]
        gc = gs_ref[pl.ds(r0, chunk), :]
        acc = jnp.zeros((chunk, d), F32)
        for e in range(epg):
            ge = jnp.sum(jnp.where(lane == g * epg + e, gc, 0.0), axis=-1, keepdims=True)
            hg = _dot(xc, wg_ref[e])
            hid = hg * jax.nn.sigmoid(hg) * _dot(xc, wu_ref[e]) * ge
            acc = acc + _dot(hid.astype(BF16), wd_ref[e])
        ys_ref[pl.ds(r0, chunk), :] += acc
        return carry

    lax.fori_loop(c_lo, c_hi, run_chunk, 0)

    @pl.when(g == ng - 1)
    def _():
        ys = ys_ref[...]
        y_hi = ys.astype(BF16)
        y_lo = (ys - y_hi.astype(F32)).astype(BF16)
        gate2 = gate2_ref[...]
        for c in range(n_chunks):
            rows = slice(c * chunk, (c + 1) * chunk)
            src = lax.broadcasted_iota(jnp.int32, (chunk, t), 1).astype(F32)
            pm = jnp.where(src == pos_ref[rows, 0:1], 1.0, 0.0).astype(BF16)
            ffn = _dot(pm, y_hi) + _dot(pm, y_lo)
            g2 = gate2 if gate2.shape[0] == 1 else gate2[rows, :]
            y_ref[rows, :] = _layer_norm(alpha * x1_ref[rows, :] + g2 * ffn, g_ref[...], b_ref[...])


def _moe_call(u2, route, x1, mods, lw, *, per_row, rows_per_batch, alpha):
    r, d = x1.shape
    ng, epg, _, f = lw["w_gate"].shape
    tm = min(1024, r)
    chunk = min(128, tm)
    tpb = max(rows_per_batch // tm, 1)
    once = lambda w: pl.BlockSpec((tm, w), lambda i, g: (i, 0), pipeline_mode=pl.Buffered(1))
    vec = pl.BlockSpec((1, d), lambda i, g: (0, 0))
    if per_row:
        gate2 = pl.BlockSpec((None, tm, d), lambda i, g: (0, i, 5))
    else:
        gate2 = pl.BlockSpec((None, 1, d), lambda i, g: (i // tpb, 0, 5))
    return pl.pallas_call(
        functools.partial(_moe_kernel, alpha=alpha, chunk=chunk),
        grid=(r // tm, ng),
        in_specs=[once(d), once(LANES),
                  pl.BlockSpec((None, epg, d, f), lambda i, g: (g, 0, 0, 0)),
                  pl.BlockSpec((None, epg, d, f), lambda i, g: (g, 0, 0, 0)),
                  pl.BlockSpec((None, epg, f, d), lambda i, g: (g, 0, 0, 0)),
                  once(d), gate2, vec, vec],
        out_specs=pl.BlockSpec((tm, d), lambda i, g: (i, 0)),
        out_shape=jax.ShapeDtypeStruct((r, d), F32),
        scratch_shapes=[pltpu.VMEM((tm, d), BF16), pltpu.VMEM((tm, LANES), F32), pltpu.VMEM((tm, d), F32),
                        pltpu.VMEM((tm, LANES), F32), pltpu.SMEM((ng + 1,), jnp.int32)],
        compiler_params=_cparams(("arbitrary", "arbitrary")),
        name="moe_ffn_ln",
    )(u2, route, lw["w_gate"], lw["w_up"], lw["w_down"], x1, mods, lw["ln2_g"], lw["ln2_b"])


def _prep_layer_weights(li, w_in, b_f, conv_w, conv_b, w_rg, b_rg, w_ig, b_ig, lru_lambda, w_br_lru, w_br_attn,
                        w_out, ln1_g, ln1_b, w_grp, b_grp, w_rt, b_rt, w_gate, w_up, w_down, ln2_g, ln2_b):
    d = w_in.shape[1]
    h = b_f.shape[1]
    wi = w_in[li]
    c_main = 5 * d
    w_f = jnp.pad(wi[:, c_main:c_main + h], ((0, 0), (0, LANES - h)))
    w_cat = jnp.concatenate([wi[:, :c_main], wi[:, c_main + h:], w_f], axis=1).astype(BF16)
    w_router = jnp.pad(jnp.concatenate([w_rt[li], w_grp[li]], axis=1), ((0, 0), (0, LANES - N_EXPERTS - N_GROUPS)))
    w_router_hi = w_router.astype(BF16)
    w_router = jnp.concatenate([w_router_hi, (w_router - w_router_hi.astype(F32)).astype(BF16)], axis=1)
    b_router = jnp.pad(jnp.concatenate([b_rt[li], b_grp[li]]), (0, LANES - N_EXPERTS - N_GROUPS)).reshape(1, LANES)
    row = lambda v: v.reshape(1, -1)
    by_group = lambda w: w.astype(BF16).reshape(N_GROUPS, EXPERTS_PER_GROUP, *w.shape[1:])
    return {
        "w_cat": w_cat, "bf_pad": jnp.pad(b_f[li], (0, LANES - h)).reshape(1, LANES),
        "conv_w": conv_w[li], "conv_b": row(conv_b[li]),
        "w_rg": w_rg[li].astype(BF16), "b_rg": row(b_rg[li]), "w_ig": w_ig[li].astype(BF16), "b_ig": row(b_ig[li]),
        "lam": row(lru_lambda[li]),
        "w_br_lru": w_br_lru[li].astype(BF16), "w_br_attn": w_br_attn[li].astype(BF16), "w_out": w_out[li].astype(BF16),
        "ln1_g": row(ln1_g[li]), "ln1_b": row(ln1_b[li]), "ln2_g": row(ln2_g[li]), "ln2_b": row(ln2_b[li]),
        "w_router": w_router, "b_router": b_router,
        "w_gate": by_group(w_gate[li]), "w_up": by_group(w_up[li]), "w_down": by_group(w_down[li]),
    }


def kernel(x_prompt, x_sample, cache_k, cache_v, cache_logf, state_conv, state_h, page_table, c_prompt, c_sample, w_mod, b_mod, w_in, b_f, conv_w, conv_b, w_rg, b_rg, w_ig, b_ig, lru_lambda, w_br_lru, w_br_attn, w_out, ln1_g, ln1_b, w_grp, b_grp, w_rt, b_rt, w_gate, w_up, w_down, ln2_g, ln2_b):
    b, t_len, d = x_prompt.shape
    db, s_len, _ = x_sample.shape
    depth = w_mod.shape[0]
    h = N_HEADS
    hd = d // h
    assert s_len == 1, "the sample group decodes one token per sequence"
    assert t_len >= CONV_W - 1
    alpha = (2.0 * depth) ** 0.25
    q_scale = hd ** -0.5 * LOG2_E

    xp = x_prompt.reshape(b * t_len, d)
    xs = x_sample.reshape(db, d)
    rc = -(-(b + db) // SUBLANES) * SUBLANES
    c_all = jnp.concatenate([c_prompt, c_sample, jnp.zeros((rc - b - db, d), F32)], axis=0)
    st_p, st_s = [], []
    for li in range(depth):
        lw = _prep_layer_weights(li, w_in, b_f, conv_w, conv_b, w_rg, b_rg, w_ig, b_ig, lru_lambda, w_br_lru,
                                 w_br_attn, w_out, ln1_g, ln1_b, w_grp, b_grp, w_rt, b_rt, w_gate, w_up, w_down,
                                 ln2_g, ln2_b)
        mods = _mods_call(c_all, w_mod[li], b_mod[li])
        mods_p = mods[:b].reshape(b, 1, 6 * d)
        mods_s = mods[b:b + db].reshape(1, db, 6 * d)

        xb, gb, qx, k, v, kx, vb, g_lru, g_att, logf, _ = _inproj_call(
            xp, mods_p, lw["w_cat"], lw["bf_pad"], per_row=False, rows_per_batch=t_len, q_scale=q_scale)
        lru, h_last = _lru_prompt_call(xb, gb, b, t_len, lw)
        att = _attn_prompt_call(qx, kx, vb, b, t_len)
        x1, u2, gates = _merge_call(xp, lru, att, g_lru, g_att, mods_p, lw, per_row=False, rows_per_batch=t_len, alpha=alpha)
        xp = _moe_call(u2, gates, x1, mods_p, lw, per_row=False, rows_per_batch=t_len, alpha=alpha)
        st_p.append((k.reshape(b, t_len, h, hd), v.reshape(b, t_len, h, hd), logf.reshape(b, t_len, h),
                     xb.reshape(b, t_len, d)[:, t_len - (CONV_W - 1):], h_last.reshape(b, d)))

        xb_s, gb_s, qx_s, k_s, v_s, _, _, gl_s, ga_s, logf_s, _ = _inproj_call(
            xs, mods_s, lw["w_cat"], lw["bf_pad"], per_row=True, rows_per_batch=1, q_scale=q_scale)
        q_s = qx_s.reshape(db, h, 2, hd)[:, :, 0, :].reshape(db, d)
        taps = [state_conv[li][:, tap, :] for tap in range(CONV_W - 1)]
        lru_s, h_new = _lru_sample_call(xb_s, gb_s, taps, state_h[li], lw)
        ct_pages = _logf_pages_call(cache_logf[li])
        att_s = _attn_sample_call(q_s, k_s, v_s, logf_s, cache_k[li], cache_v[li], ct_pages, page_table)
        x1_s, u2_s, gates_s = _merge_call(xs, lru_s, att_s.astype(BF16), gl_s, ga_s, mods_s, lw,
                                          per_row=True, rows_per_batch=1, alpha=alpha)
        xs = _moe_call(u2_s, gates_s, x1_s, mods_s, lw, per_row=True, rows_per_batch=1, alpha=alpha)
        conv_new = jnp.concatenate([state_conv[li][:, 1:], xb_s[:, None, :]], axis=1)
        st_s.append((k_s.reshape(db, 1, h, hd), v_s.reshape(db, 1, h, hd), logf_s.reshape(db, 1, h), conv_new, h_new))

    stack = lambda sts, i: jnp.stack([s[i] for s in sts])
    return (xp.reshape(b, t_len, d), xs.reshape(db, 1, d),
            stack(st_p, 0), stack(st_p, 1), stack(st_p, 2), stack(st_p, 3), stack(st_p, 4),
            stack(st_s, 0), stack(st_s, 1), stack(st_s, 2), stack(st_s, 3), stack(st_s, 4))
```

```python
import functools
import math

import jax
import jax.numpy as jnp
from jax import lax
from jax.experimental import pallas as pl
from jax.experimental.pallas import tpu as pltpu

F32 = jnp.float32
BF16 = jnp.bfloat16

N_HEADS = 8
N_RNN_BLOCKS = 8
CONV_W = 4
LRU_C = 8.0
N_GROUPS = 4
EXPERTS_PER_GROUP = 8
N_EXPERTS = N_GROUPS * EXPERTS_PER_GROUP
GROUP_LANE = N_EXPERTS
LN_EPS = 1e-5
LANES = 128
SUBLANES = 8
MXU_N = 256
VMEM_LIMIT = 56 * 1024 * 1024

NEG_INF = float("-inf")
LOG2_E = math.log2(math.e)


def _cparams(sem):
    return pltpu.CompilerParams(dimension_semantics=sem, vmem_limit_bytes=VMEM_LIMIT)


def _split3(x):
    hi = x.astype(BF16)
    r1 = x - hi.astype(F32)
    mid = r1.astype(BF16)
    lo = (r1 - mid.astype(F32)).astype(BF16)
    return hi, mid, lo


def _dot(a, b):
    return jnp.dot(a, b, preferred_element_type=F32)


def _layer_norm(z, g, b):
    mu = jnp.mean(z, axis=-1, keepdims=True)
    zc = z - mu
    var = jnp.mean(zc * zc, axis=-1, keepdims=True)
    return zc * lax.rsqrt(var + LN_EPS) * g + b


def _log_sigmoid(x):
    return jnp.minimum(x, 0.0) - jnp.log1p(jnp.exp(-jnp.abs(x)))


def _softplus(x):
    return jnp.maximum(x, 0.0) + jnp.log1p(jnp.exp(-jnp.abs(x)))


def _gelu_tanh(x):
    c = math.sqrt(2.0 / math.pi)
    return 0.5 * x * (1.0 + jnp.tanh(c * (x + 0.044715 * (x * x * x))))


def _mod_spec(per_row, tm, d, chunk, tiles_per_batch):
    if per_row:
        return pl.BlockSpec((None, tm, d), lambda i: (0, i, chunk))
    return pl.BlockSpec((None, 1, d), lambda i: (i // tiles_per_batch, 0, chunk))


def _mods_kernel(c_ref, w_ref, b_ref, o_ref):
    c = c_ref[...]
    s = (c * jax.nn.sigmoid(c)).astype(BF16)
    o_ref[...] = _dot(s, w_ref[...].astype(BF16)) + b_ref[...]


def _mods_call(c_all, w_mod, b_mod):
    rc, d = c_all.shape
    n = w_mod.shape[1]
    tn = 1536
    return pl.pallas_call(
        _mods_kernel,
        grid=(n // tn,),
        in_specs=[pl.BlockSpec((rc, d), lambda j: (0, 0)),
                  pl.BlockSpec((d, tn), lambda j: (0, j)),
                  pl.BlockSpec((1, tn), lambda j: (0, j))],
        out_specs=pl.BlockSpec((rc, tn), lambda j: (0, j)),
        out_shape=jax.ShapeDtypeStruct((rc, n), F32),
        compiler_params=_cparams(("arbitrary",)),
        name="adaln_mods",
    )(c_all, w_mod, b_mod.reshape(1, n))


def _inproj_kernel(x_ref, shift_ref, scale_ref, w_ref, bf_ref, tri_ref, sel_ref,
                   xb_ref, gb_ref, qx_ref, k_ref, v_ref, kx_ref, vb_ref, gl_ref, ga_ref, logf_ref, cum_ref,
                   carry_ref, *, tiles_per_batch, q_scale):
    i = pl.program_id(0)
    d = x_ref.shape[1]
    tm = x_ref.shape[0]
    hd = d // N_HEADS
    u = (x_ref[...] * (1.0 + scale_ref[...]) + shift_ref[...]).astype(BF16)

    def mm(c):
        return _dot(u, w_ref[:, c * d:(c + 1) * d])

    def store_heads(ref, main, extra):
        for h in range(N_HEADS):
            ref[:, 2 * h * hd:(2 * h + 1) * hd] = main[:, h * hd:(h + 1) * hd]
            ref[:, (2 * h + 1) * hd:(2 * h + 2) * hd] = extra if extra.shape[1] == hd else extra[:, h * hd:(h + 1) * hd]

    xb_ref[...] = mm(0)
    gb_ref[...] = mm(1)
    three_ones = jnp.where(lax.broadcasted_iota(jnp.int32, (tm, hd), 1) < 3, 1.0, 0.0).astype(BF16)
    store_heads(qx_ref, (mm(2) * q_scale).astype(BF16), three_ones)
    k = mm(3)
    k_ref[...] = k
    v = mm(4)
    v_ref[...] = v
    vb_ref[...] = v.astype(BF16)
    gl_ref[...] = mm(5)
    ga_ref[...] = mm(6)
    fg = _dot(u, w_ref[:, 7 * d:7 * d + LANES]) + bf_ref[...]
    logf = _log_sigmoid(fg)
    logf_ref[...] = logf[:, :N_HEADS]

    @pl.when(i % tiles_per_batch == 0)
    def _():
        carry_ref[...] = jnp.zeros_like(carry_ref)

    tri = tri_ref[...]
    hi, mid, lo = _split3(logf)
    cum = _dot(tri, hi) + _dot(tri, mid) + _dot(tri, lo) + carry_ref[...]
    cum_ref[...] = cum[:, :N_HEADS]
    carry_ref[...] = cum[tm - 1:tm, :]
    terms = jnp.concatenate(_split3(cum * (-LOG2_E)), axis=1)
    store_heads(kx_ref, k.astype(BF16), _dot(terms, sel_ref[...]).astype(BF16))


def _inproj_call(x, mods, w_cat, bf_pad, *, per_row, rows_per_batch, q_scale):
    r, d = x.shape
    hd = d // N_HEADS
    tm = min(256, r)
    tiles_per_batch = max(rows_per_batch // tm, 1)
    tri = (jnp.arange(tm)[:, None] >= jnp.arange(tm)[None, :]).astype(BF16)
    src = jnp.arange(3 * LANES)
    dst = jnp.arange(d)
    sel = ((src[:, None] % LANES == dst[None, :] // hd) & (src[:, None] // LANES == dst[None, :] % hd)).astype(BF16)
    row_f32 = pl.BlockSpec((tm, d), lambda i: (i, 0))
    wide = pl.BlockSpec((tm, 2 * d), lambda i: (i, 0))
    small = pl.BlockSpec((tm, N_HEADS), lambda i: (i, 0))
    sd = lambda dt: jax.ShapeDtypeStruct((r, d), dt)
    sw = jax.ShapeDtypeStruct((r, 2 * d), BF16)
    sh = jax.ShapeDtypeStruct((r, N_HEADS), F32)
    return pl.pallas_call(
        functools.partial(_inproj_kernel, tiles_per_batch=tiles_per_batch, q_scale=q_scale),
        grid=(r // tm,),
        in_specs=[row_f32,
                  _mod_spec(per_row, tm, d, 0, tiles_per_batch),
                  _mod_spec(per_row, tm, d, 1, tiles_per_batch),
                  pl.BlockSpec(w_cat.shape, lambda i: (0, 0), pipeline_mode=pl.Buffered(1)),
                  pl.BlockSpec((1, LANES), lambda i: (0, 0)),
                  pl.BlockSpec((tm, tm), lambda i: (0, 0)),
                  pl.BlockSpec((3 * LANES, d), lambda i: (0, 0))],
        out_specs=[row_f32, row_f32, wide, row_f32, row_f32, wide, row_f32, row_f32, row_f32, small, small],
        out_shape=[sd(F32), sd(F32), sw, sd(F32), sd(F32), sw, sd(BF16), sd(F32), sd(F32), sh, sh],
        scratch_shapes=[pltpu.VMEM((1, LANES), F32)],
        compiler_params=_cparams(("arbitrary",)),
        name="in_projection",
    )(x, mods, mods, w_cat, bf_pad, tri, sel)


def _lru_gates(xc, wrg_ref, brg, wig_ref, big, lam):
    blk = xc.shape[1] // N_RNN_BLOCKS
    rs, is_ = [], []
    for n in range(N_RNN_BLOCKS):
        xn = xc[:, n * blk:(n + 1) * blk].astype(BF16)
        rs.append(_dot(xn, wrg_ref[n]))
        is_.append(_dot(xn, wig_ref[n]))
    r = jax.nn.sigmoid(jnp.concatenate(rs, axis=1) + brg)
    ig = jax.nn.sigmoid(jnp.concatenate(is_, axis=1) + big)
    log_a = (-LRU_C) * r * _softplus(-lam)
    a = jnp.exp(log_a)
    mult = jnp.sqrt(1.0 - jnp.exp(2.0 * log_a))
    return a, mult, ig


def _lru_prompt_kernel(xb_ref, gb_ref, cw_ref, cb_ref, wrg_ref, brg_ref, wig_ref, big_ref, lam_ref,
                       out_ref, hlast_ref, xp_ref, a_ref, b_ref, hs_ref, h_ref):
    t = pl.program_id(1)
    nt = pl.num_programs(1)
    tm = xb_ref.shape[0]
    halo = SUBLANES

    @pl.when(t == 0)
    def _():
        xp_ref[0:halo, :] = jnp.zeros((halo, xp_ref.shape[1]), F32)
        h_ref[...] = jnp.zeros_like(h_ref)

    xp_ref[halo:halo + tm, :] = xb_ref[...]
    xc = cb_ref[...]
    for tap in range(CONV_W):
        off = halo - (CONV_W - 1) + tap
        xc = xc + xp_ref[off:off + tm, :] * cw_ref[tap:tap + 1, :]
    a, mult, ig = _lru_gates(xc, wrg_ref, brg_ref[...], wig_ref, big_ref[...], lam_ref[...])
    row = lax.broadcasted_iota(jnp.int32, (tm, 1), 0)
    mult = jnp.where((row == 0) & (t == 0), 1.0, mult)
    a_ref[...] = a
    b_ref[...] = mult * ig * xc

    def step(s, h):
        h = a_ref[pl.ds(s, 1), :] * h + b_ref[pl.ds(s, 1), :]
        hs_ref[pl.ds(s, 1), :] = h
        return h

    h = lax.fori_loop(0, tm, step, h_ref[...], unroll=8)
    h_ref[...] = h
    xp_ref[halo - (CONV_W - 1):halo, :] = xp_ref[halo + tm - (CONV_W - 1):halo + tm, :]
    out_ref[...] = (hs_ref[...] * _gelu_tanh(gb_ref[...])).astype(BF16)

    @pl.when(t == nt - 1)
    def _():
        hlast_ref[...] = h


def _lru_prompt_call(xb, gb, b, t_len, lw):
    r, d = xb.shape
    tm = min(256, t_len)
    nt = t_len // tm
    row = pl.BlockSpec((tm, d), lambda bi, ti: (bi * nt + ti, 0))
    vec = pl.BlockSpec((1, d), lambda bi, ti: (0, 0))
    blk = d // N_RNN_BLOCKS
    wspec = pl.BlockSpec((N_RNN_BLOCKS, blk, blk), lambda bi, ti: (0, 0, 0))
    return pl.pallas_call(
        _lru_prompt_kernel,
        grid=(b, nt),
        in_specs=[row, row, pl.BlockSpec((CONV_W, d), lambda bi, ti: (0, 0)), vec, wspec, vec, wspec, vec, vec],
        out_specs=[row, pl.BlockSpec((None, 1, d), lambda bi, ti: (bi, 0, 0))],
        out_shape=[jax.ShapeDtypeStruct((r, d), BF16), jax.ShapeDtypeStruct((b, 1, d), F32)],
        scratch_shapes=[pltpu.VMEM((tm + SUBLANES, d), F32), pltpu.VMEM((tm, d), F32), pltpu.VMEM((tm, d), F32),
                        pltpu.VMEM((tm, d), F32), pltpu.VMEM((1, d), F32)],
        compiler_params=_cparams(("arbitrary", "arbitrary")),
        name="rglru_prompt",
    )(xb, gb, lw["conv_w"], lw["conv_b"], lw["w_rg"], lw["b_rg"], lw["w_ig"], lw["b_ig"], lw["lam"])


def _lru_sample_kernel(xb_ref, gb_ref, c0_ref, c1_ref, c2_ref, hprev_ref, cw_ref, cb_ref,
                       wrg_ref, brg_ref, wig_ref, big_ref, lam_ref, out_ref, hnew_ref):
    xb = xb_ref[...]
    xc = (cb_ref[...] + c0_ref[...] * cw_ref[0:1, :] + c1_ref[...] * cw_ref[1:2, :]
          + c2_ref[...] * cw_ref[2:3, :] + xb * cw_ref[3:4, :])
    a, mult, ig = _lru_gates(xc, wrg_ref, brg_ref[...], wig_ref, big_ref[...], lam_ref[...])
    h = a * hprev_ref[...] + mult * ig * xc
    hnew_ref[...] = h
    out_ref[...] = (h * _gelu_tanh(gb_ref[...])).astype(BF16)


def _lru_sample_call(xb, gb, conv_taps, h_prev, lw):
    r, d = xb.shape
    return pl.pallas_call(
        _lru_sample_kernel,
        out_shape=[jax.ShapeDtypeStruct((r, d), BF16), jax.ShapeDtypeStruct((r, d), F32)],
        compiler_params=pltpu.CompilerParams(vmem_limit_bytes=VMEM_LIMIT),
        name="rglru_sample",
    )(xb, gb, conv_taps[0], conv_taps[1], conv_taps[2], h_prev, lw["conv_w"], lw["conv_b"],
      lw["w_rg"], lw["b_rg"], lw["w_ig"], lw["b_ig"], lw["lam"])


def _attn_prompt_kernel(qi_ref, ki_ref, q_ref, k_ref, v_ref, o_ref, m_ref, acc_ref):
    p_id = pl.program_id(1)
    qi = qi_ref[p_id]
    ki = ki_ref[p_id]
    tq, tk = q_ref.shape[0], k_ref.shape[0]
    hd = v_ref.shape[1] // N_HEADS

    @pl.when(ki == 0)
    def _():
        m_ref[...] = jnp.full_like(m_ref, NEG_INF)
        acc_ref[...] = jnp.zeros_like(acc_ref)

    def update(masked):
        ones = jnp.ones((tk, hd), BF16)
        if masked:
            keep = (lax.broadcasted_iota(jnp.int32, (tq, tk), 0) >= lax.broadcasted_iota(jnp.int32, (tq, tk), 1))
        for h in range(N_HEADS):
            sl = slice(h * hd, (h + 1) * hd)
            sl2 = slice(2 * h * hd, 2 * (h + 1) * hd)
            s = lax.dot_general(q_ref[:, sl2], k_ref[:, sl2], (((1,), (1,)), ((), ())), preferred_element_type=F32)
            if masked:
                s = jnp.where(keep, s, NEG_INF)
            m_prev = m_ref[h]
            m_new = jnp.maximum(m_prev, jnp.max(s, axis=-1, keepdims=True))
            p = jnp.exp2(s - m_new).astype(BF16)
            pv = _dot(p, jnp.concatenate([v_ref[:, sl], ones], axis=1))
            acc_ref[h] = jnp.exp2(m_prev - m_new) * acc_ref[h] + pv
            m_ref[h] = m_new

    @pl.when(ki < qi)
    def _():
        update(False)

    @pl.when(ki == qi)
    def _():
        update(True)
        for h in range(N_HEADS):
            a = acc_ref[h]
            o_ref[:, h * hd:(h + 1) * hd] = (a[:, :hd] / a[:, hd:]).astype(o_ref.dtype)


def _attn_prompt_call(qx, kx, vb, b, t_len):
    r, d = vb.shape
    hd = d // N_HEADS
    tq = min(512, t_len)
    nq = t_len // tq
    pairs = [(a, c) for a in range(nq) for c in range(a + 1)]
    qi_tab = jnp.asarray([p[0] for p in pairs], jnp.int32)
    ki_tab = jnp.asarray([p[1] for p in pairs], jnp.int32)
    grid_spec = pltpu.PrefetchScalarGridSpec(
        num_scalar_prefetch=2,
        grid=(b, len(pairs)),
        in_specs=[pl.BlockSpec((tq, 2 * d), lambda bi, p, qt, kt: (bi * nq + qt[p], 0)),
                  pl.BlockSpec((tq, 2 * d), lambda bi, p, qt, kt: (bi * nq + kt[p], 0)),
                  pl.BlockSpec((tq, d), lambda bi, p, qt, kt: (bi * nq + kt[p], 0))],
        out_specs=pl.BlockSpec((tq, d), lambda bi, p, qt, kt: (bi * nq + qt[p], 0)),
        scratch_shapes=[pltpu.VMEM((N_HEADS, tq, 1), F32), pltpu.VMEM((N_HEADS, tq, 2 * hd), F32)],
    )
    return pl.pallas_call(
        _attn_prompt_kernel,
        grid_spec=grid_spec,
        out_shape=jax.ShapeDtypeStruct((r, d), BF16),
        compiler_params=_cparams(("arbitrary", "arbitrary")),
        name="fox_attention_prompt",
    )(qi_tab, ki_tab, qx, kx, vb)


def _logf_pages_kernel(l_ref, u_ref, ones_ref, pe_ref, po_ref, o_ref):
    page = l_ref.shape[1]
    parts = _split3(l_ref[...])
    insuf = sum(_dot(p, u_ref[...]) for p in parts) * LOG2_E
    hi = insuf.astype(BF16)
    mid = (insuf - hi.astype(F32)).astype(BF16)
    o_ref[:, 0:2 * page] = _dot(hi, pe_ref[...]) + _dot(mid, po_ref[...])
    o_ref[:, 2 * page:3 * page] = sum(_dot(p, ones_ref[...]) for p in parts) * LOG2_E


def _logf_pages_call(cache_logf):
    n_pool, page, h = cache_logf.shape
    rows = n_pool * h
    lft = jnp.swapaxes(cache_logf, 1, 2).reshape(rows, page)
    kk = jnp.arange(page)
    later = (kk[:, None] > kk[None, :]).astype(BF16)
    ones = jnp.ones((page, page), BF16)
    pe = (2 * kk[:, None] == jnp.arange(2 * page)[None, :]).astype(BF16)
    po = (2 * kk[:, None] + 1 == jnp.arange(2 * page)[None, :]).astype(BF16)
    tr = 2048
    while rows % tr:
        tr //= 2
    const = lambda shape: pl.BlockSpec(shape, lambda i: (0, 0))
    out = pl.pallas_call(
        _logf_pages_kernel,
        grid=(rows // tr,),
        in_specs=[pl.BlockSpec((tr, page), lambda i: (i, 0)), const((page, page)), const((page, page)),
                  const((page, 2 * page)), const((page, 2 * page))],
        out_specs=pl.BlockSpec((tr, 3 * page), lambda i: (i, 0)),
        out_shape=jax.ShapeDtypeStruct((rows, 3 * page), F32),
        compiler_params=_cparams(("arbitrary",)),
        name="logf_page_sums",
    )(lft, later, ones, pe, po)
    return out.reshape(n_pool, h, 3 * page)


def _attn_sample_kernel(pt_ref, q_ref, knew_ref, vnew_ref, cq_ref, ones_ref, *refs, pages_per_step):
    pp = pages_per_step
    k_refs, v_refs, ct_refs = refs[0:pp], refs[pp:2 * pp], refs[2 * pp:3 * pp]
    o_ref = refs[3 * pp]
    m_ref, l_ref, acc_ref, psuf_ref, mstep_ref, s_ref, psufs_ref = refs[3 * pp + 1:]
    j = pl.program_id(1)
    nj = pl.num_programs(1)
    page, h, hd = k_refs[0].shape
    q = q_ref[...]

    def lane_sums(lhs):
        return _dot(lhs.astype(BF16), ones_ref[...])

    @pl.when(j == 0)
    def _():
        prod = jnp.concatenate([knew_ref[...] * q, jnp.zeros((h, hd), F32)], axis=1)
        prod = jnp.concatenate([prod, jnp.zeros_like(prod)], axis=0)
        m_ref[...] = lane_sums(prod)[:h]
        l_ref[...] = jnp.ones_like(l_ref)
        acc_ref[...] = vnew_ref[...]
        psuf_ref[...] = cq_ref[...]
        mstep_ref[...] = jnp.full_like(mstep_ref, NEG_INF)
        s_ref[...] = jnp.full_like(s_ref, NEG_INF)
        psufs_ref[...] = jnp.zeros_like(psufs_ref)

    m_old = m_ref[...]
    m_new = jnp.maximum(m_old, mstep_ref[...])
    corr = jnp.exp2(m_old - m_new)
    l_new = l_ref[...] * corr
    acc = acc_ref[...] * corr
    for i in range(pp):
        p = jnp.exp2(s_ref[i] - (m_new - psufs_ref[i])[None])
        l_new = l_new + jnp.sum(p, axis=0)
        acc = acc + jnp.sum(p * v_refs[i][...], axis=0)
    l_ref[...] = l_new
    acc_ref[...] = acc
    m_ref[...] = m_new

    half = page // 2
    key = lax.broadcasted_iota(jnp.int32, (half, h, hd), 0)
    lane = lax.broadcasted_iota(jnp.int32, (half, h, hd), 2)
    own = (lane >> 1) == key
    psuf = psuf_ref[...]
    mstep = jnp.full((h, hd), NEG_INF, F32)
    for i in range(pp):
        prod = k_refs[i][...] * q[None]
        ct = ct_refs[i][...]
        bias = jnp.concatenate([jnp.where(own, ct[None, :, :hd], 0.0),
                                jnp.where(own, ct[None, :, hd:2 * hd], 0.0)], axis=0)
        lhs = jnp.concatenate([prod, bias], axis=2).reshape(page * h, 2 * hd)
        s = lane_sums(lhs).reshape(page, h, hd)
        mstep = jnp.maximum(mstep, jnp.max(s, axis=0) + psuf)
        s_ref[i] = s
        psufs_ref[i] = psuf
        psuf = psuf + ct[:, 2 * hd:]
    mstep_ref[...] = mstep
    psuf_ref[...] = psuf

    @pl.when(j == nj - 1)
    def _():
        o_ref[...] = acc_ref[...] / l_ref[...]


def _attn_sample_call(q, k_new, v_new, logf_new, cache_k, cache_v, ct_pages, page_table):
    db, d = q.shape
    n_pool, page, h, hd = cache_k.shape
    assert page == hd, "the two bias terms of a page's keys are packed into 2*HEAD_DIM lanes"
    n_pages = page_table.shape[1]
    pp = 16
    while n_pages % pp:
        pp //= 2
    n_steps = n_pages // pp
    cq = jnp.broadcast_to((logf_new * LOG2_E)[:, :, None], (db, h, hd))
    ones = jnp.ones((2 * hd, hd), BF16)

    def key_page(i, tail):
        def index(bi, j, pt):
            step = jnp.minimum(j, n_steps - 1)
            return (pt[bi, n_pages - 1 - (step * pp + i)],) + tail
        return index

    def value_page(i):
        def index(bi, j, pt):
            step = jnp.maximum(j - 1, 0)
            return (pt[bi, n_pages - 1 - (step * pp + i)], 0, 0, 0)
        return index

    tile = pl.BlockSpec((None, h, hd), lambda bi, j, pt: (bi, 0, 0))
    in_specs = [tile, tile, tile, tile, pl.BlockSpec((2 * hd, hd), lambda bi, j, pt: (0, 0))]
    in_specs += [pl.BlockSpec((None, page, h, hd), key_page(i, (0, 0, 0))) for i in range(pp)]
    in_specs += [pl.BlockSpec((None, page, h, hd), value_page(i)) for i in range(pp)]
    in_specs += [pl.BlockSpec((None, h, 3 * hd), key_page(i, (0, 0))) for i in range(pp)]
    grid_spec = pltpu.PrefetchScalarGridSpec(
        num_scalar_prefetch=1,
        grid=(db, n_steps + 1),
        in_specs=in_specs,
        out_specs=tile,
        scratch_shapes=[pltpu.VMEM((h, hd), F32)] * 5 + [pltpu.VMEM((pp, page, h, hd), F32),
                                                         pltpu.VMEM((pp, h, hd), F32)],
    )
    out = pl.pallas_call(
        functools.partial(_attn_sample_kernel, pages_per_step=pp),
        grid_spec=grid_spec,
        out_shape=jax.ShapeDtypeStruct((db, h, hd), F32),
        compiler_params=_cparams(("arbitrary", "arbitrary")),
        name="fox_attention_sample",
    )(page_table, q.astype(F32).reshape(db, h, hd), k_new.reshape(db, h, hd), v_new.reshape(db, h, hd), cq, ones,
      *([cache_k] * pp), *([cache_v] * pp), *([ct_pages] * pp))
    return out.reshape(db, d)


def _router_gates(logits):
    lane = lax.broadcasted_iota(jnp.int32, logits.shape, 1)
    lane_f = lane.astype(F32)
    big = float(4 * LANES)
    is_grp = (lane >= N_EXPERTS) & (lane < N_EXPERTS + N_GROUPS)
    gl = jnp.where(is_grp, logits, NEG_INF)
    gmax = jnp.max(gl, axis=-1, keepdims=True)
    gidx = jnp.min(jnp.where(gl == gmax, lane_f - N_EXPERTS, big), axis=-1, keepdims=True)
    g_p = 1.0 / jnp.sum(jnp.exp(gl - gmax), axis=-1, keepdims=True)
    in_grp = (lane < N_EXPERTS) & ((lane // EXPERTS_PER_GROUP).astype(F32) == gidx)
    e1 = jnp.where(in_grp, logits, NEG_INF)
    m1 = jnp.max(e1, axis=-1, keepdims=True)
    i1 = jnp.min(jnp.where(e1 == m1, lane_f, big), axis=-1, keepdims=True)
    e2 = jnp.where(lane_f == i1, NEG_INF, e1)
    m2 = jnp.max(e2, axis=-1, keepdims=True)
    i2 = jnp.min(jnp.where(e2 == m2, lane_f, big), axis=-1, keepdims=True)
    dlt = jnp.exp(m2 - m1)
    w1 = g_p / (1.0 + dlt)
    w2 = g_p * dlt / (1.0 + dlt)
    gates = jnp.where(lane_f == i1, w1, 0.0) + jnp.where(lane_f == i2, w2, 0.0)
    return jnp.where(lane == GROUP_LANE, gidx, gates)


def _merge_kernel(x_ref, lru_ref, att_ref, gl_ref, ga_ref, gate1_ref, shift2_ref, scale2_ref,
                  wl_ref, wa_ref, wo_ref, g_ref, b_ref, wr_ref, br_ref,
                  x1_ref, u2_ref, gates_ref, *, alpha):
    y = (jax.nn.sigmoid(gl_ref[...]) * _dot(lru_ref[...], wl_ref[...])
         + jax.nn.sigmoid(ga_ref[...]) * _dot(att_ref[...], wa_ref[...]))
    mix = _dot(y.astype(BF16), wo_ref[...])
    x1 = _layer_norm(alpha * x_ref[...] + gate1_ref[...] * mix, g_ref[...], b_ref[...])
    x1_ref[...] = x1
    u2 = x1 * (1.0 + scale2_ref[...]) + shift2_ref[...]
    u_hi = u2.astype(BF16)
    u2_ref[...] = u_hi
    u_lo = (u2 - u_hi.astype(F32)).astype(BF16)
    both = _dot(u_hi, wr_ref[...])
    logits = both[:, :LANES] + both[:, LANES:] + _dot(u_lo, wr_ref[:, :LANES]) + br_ref[...]
    gates_ref[...] = _router_gates(logits)


def _merge_call(x, lru, att, g_lru, g_att, mods, lw, *, per_row, rows_per_batch, alpha):
    r, d = x.shape
    tm = min(512, r)
    tpb = max(rows_per_batch // tm, 1)
    row = pl.BlockSpec((tm, d), lambda i: (i, 0))
    wsq = pl.BlockSpec((d, d), lambda i: (0, 0))
    vec = pl.BlockSpec((1, d), lambda i: (0, 0))
    return pl.pallas_call(
        functools.partial(_merge_kernel, alpha=alpha),
        grid=(r // tm,),
        in_specs=[row, row, row, row, row,
                  _mod_spec(per_row, tm, d, 2, tpb), _mod_spec(per_row, tm, d, 3, tpb), _mod_spec(per_row, tm, d, 4, tpb),
                  wsq, wsq, wsq, vec, vec,
                  pl.BlockSpec((d, 2 * LANES), lambda i: (0, 0)), pl.BlockSpec((1, LANES), lambda i: (0, 0))],
        out_specs=[row, row, pl.BlockSpec((tm, LANES), lambda i: (i, 0))],
        out_shape=[jax.ShapeDtypeStruct((r, d), F32), jax.ShapeDtypeStruct((r, d), BF16),
                   jax.ShapeDtypeStruct((r, LANES), F32)],
        compiler_params=_cparams(("arbitrary",)),
        name="merge_ln_router",
    )(x, lru, att, g_lru, g_att, mods, mods, mods, lw["w_br_lru"], lw["w_br_attn"], lw["w_out"],
      lw["ln1_g"], lw["ln1_b"], lw["w_router"], lw["b_router"])


def _moe_kernel(u_ref, route_ref, wg_ref, wu_ref, wd_ref, x1_ref, gate2_ref, g_ref, b_ref, y_ref,
                xs_ref, gs_ref, ys_ref, pos_ref, off_ref, *, alpha, chunk):
    g = pl.program_id(1)
    ng = pl.num_programs(1)
    t, d = u_ref.shape
    n_chunks = t // chunk
    epg = wg_ref.shape[0]
    lane = lax.broadcasted_iota(jnp.int32, (chunk, LANES), 1)

    @pl.when(g == 0)
    def _():
        tri = jnp.where(lax.broadcasted_iota(jnp.int32, (chunk, chunk), 0)
                        >= lax.broadcasted_iota(jnp.int32, (chunk, chunk), 1), 1.0, 0.0).astype(BF16)
        carry = jnp.zeros((1, LANES), F32)
        seen = []
        for c in range(n_chunks):
            grp = route_ref[c * chunk:(c + 1) * chunk, GROUP_LANE:GROUP_LANE + 1]
            mine = lane.astype(F32) == grp
            cum = _dot(tri, jnp.where(mine, 1.0, 0.0).astype(BF16)) + carry
            carry = cum[chunk - 1:chunk, :]
            seen.append((mine, cum))
        counts = jnp.broadcast_to(carry, (SUBLANES, LANES))
        lane8 = lax.broadcasted_iota(jnp.int32, (SUBLANES, LANES), 1)
        off = jnp.zeros((SUBLANES, LANES), F32)
        for sft in range(1, ng):
            off = off + jnp.where(lane8 >= sft, pltpu.roll(counts, sft, axis=1), 0.0)
        for c, (mine, cum) in enumerate(seen):
            pos = jnp.sum(jnp.where(mine, off[0:1, :] + cum - 1.0, 0.0), axis=-1, keepdims=True)
            pos_ref[c * chunk:(c + 1) * chunk, :] = jnp.broadcast_to(pos, (chunk, LANES))
        for gg in range(ng):
            off_ref[gg] = jnp.sum(jnp.where(lane8[0:1] == gg, off[0:1], 0.0)).astype(jnp.int32)
        off_ref[ng] = t
        pos_row = pos_ref[...].T[0:1, :]
        u = u_ref[...]
        r_parts = _split3(route_ref[...])
        for c in range(n_chunks):
            rows = slice(c * chunk, (c + 1) * chunk)
            dst = (lax.broadcasted_iota(jnp.int32, (chunk, t), 0) + c * chunk).astype(F32)
            pm = jnp.where(dst == pos_row, 1.0, 0.0).astype(BF16)
            xs_ref[rows, :] = _dot(pm, u).astype(BF16)
            gs_ref[rows, :] = sum(_dot(pm, p) for p in r_parts)
        ys_ref[...] = jnp.zeros_like(ys_ref)

    c_lo = off_ref[g] // chunk
    c_hi = (off_ref[g + 1] + chunk - 1) // chunk

    def run_chunk(c, carry):
        r0 = pl.multiple_of(c * chunk, chunk)
        xc = xs_ref[pl.ds(r0, chunk), :]
        gc = gs_ref[pl.ds(r0, chunk), :]
        acc = jnp.zeros((chunk, d), F32)
        for e in range(epg):
            ge = jnp.sum(jnp.where(lane == g * epg + e, gc, 0.0), axis=-1, keepdims=True)
            hg = _dot(xc, wg_ref[e])
            hid = hg * jax.nn.sigmoid(hg) * _dot(xc, wu_ref[e]) * ge
            acc = acc + _dot(hid.astype(BF16), wd_ref[e])
        ys_ref[pl.ds(r0, chunk), :] += acc
        return carry

    lax.fori_loop(c_lo, c_hi, run_chunk, 0)

    @pl.when(g == ng - 1)
    def _():
        ys = ys_ref[...]
        y_hi = ys.astype(BF16)
        y_lo = (ys - y_hi.astype(F32)).astype(BF16)
        gate2 = gate2_ref[...]
        for c in range(n_chunks):
            rows = slice(c * chunk, (c + 1) * chunk)
            src = lax.broadcasted_iota(jnp.int32, (chunk, t), 1).astype(F32)
            pm = jnp.where(src == pos_ref[rows, 0:1], 1.0, 0.0).astype(BF16)
            ffn = _dot(pm, y_hi) + _dot(pm, y_lo)
            g2 = gate2 if gate2.shape[0] == 1 else gate2[rows, :]
            y_ref[rows, :] = _layer_norm(alpha * x1_ref[rows, :] + g2 * ffn, g_ref[...], b_ref[...])


def _moe_call(u2, route, x1, mods, lw, *, per_row, rows_per_batch, alpha):
    r, d = x1.shape
    ng, epg, _, f = lw["w_gate"].shape
    tm = min(1024, r)
    chunk = min(128, tm)
    tpb = max(rows_per_batch // tm, 1)
    once = lambda w: pl.BlockSpec((tm, w), lambda i, g: (i, 0), pipeline_mode=pl.Buffered(1))
    vec = pl.BlockSpec((1, d), lambda i, g: (0, 0))
    if per_row:
        gate2 = pl.BlockSpec((None, tm, d), lambda i, g: (0, i, 5))
    else:
        gate2 = pl.BlockSpec((None, 1, d), lambda i, g: (i // tpb, 0, 5))
    return pl.pallas_call(
        functools.partial(_moe_kernel, alpha=alpha, chunk=chunk),
        grid=(r // tm, ng),
        in_specs=[once(d), once(LANES),
                  pl.BlockSpec((None, epg, d, f), lambda i, g: (g, 0, 0, 0)),
                  pl.BlockSpec((None, epg, d, f), lambda i, g: (g, 0, 0, 0)),
                  pl.BlockSpec((None, epg, f, d), lambda i, g: (g, 0, 0, 0)),
                  once(d), gate2, vec, vec],
        out_specs=pl.BlockSpec((tm, d), lambda i, g: (i, 0)),
        out_shape=jax.ShapeDtypeStruct((r, d), F32),
        scratch_shapes=[pltpu.VMEM((tm, d), BF16), pltpu.VMEM((tm, LANES), F32), pltpu.VMEM((tm, d), F32),
                        pltpu.VMEM((tm, LANES), F32), pltpu.SMEM((ng + 1,), jnp.int32)],
        compiler_params=_cparams(("arbitrary", "arbitrary")),
        name="moe_ffn_ln",
    )(u2, route, lw["w_gate"], lw["w_up"], lw["w_down"], x1, mods, lw["ln2_g"], lw["ln2_b"])


def _prep_layer_weights(li, w_in, b_f, conv_w, conv_b, w_rg, b_rg, w_ig, b_ig, lru_lambda, w_br_lru, w_br_attn,
                        w_out, ln1_g, ln1_b, w_grp, b_grp, w_rt, b_rt, w_gate, w_up, w_down, ln2_g, ln2_b):
    d = w_in.shape[1]
    h = b_f.shape[1]
    wi = w_in[li]
    c_main = 5 * d
    w_f = jnp.pad(wi[:, c_main:c_main + h], ((0, 0), (0, LANES - h)))
    w_cat = jnp.concatenate([wi[:, :c_main], wi[:, c_main + h:], w_f], axis=1).astype(BF16)
    w_router = jnp.pad(jnp.concatenate([w_rt[li], w_grp[li]], axis=1), ((0, 0), (0, LANES - N_EXPERTS - N_GROUPS)))
    w_router_hi = w_router.astype(BF16)
    w_router = jnp.concatenate([w_router_hi, (w_router - w_router_hi.astype(F32)).astype(BF16)], axis=1)
    b_router = jnp.pad(jnp.concatenate([b_rt[li], b_grp[li]]), (0, LANES - N_EXPERTS - N_GROUPS)).reshape(1, LANES)
    row = lambda v: v.reshape(1, -1)
    by_group = lambda w: w.astype(BF16).reshape(N_GROUPS, EXPERTS_PER_GROUP, *w.shape[1:])
    return {
        "w_cat": w_cat, "bf_pad": jnp.pad(b_f[li], (0, LANES - h)).reshape(1, LANES),
        "conv_w": conv_w[li], "conv_b": row(conv_b[li]),
        "w_rg": w_rg[li].astype(BF16), "b_rg": row(b_rg[li]), "w_ig": w_ig[li].astype(BF16), "b_ig": row(b_ig[li]),
        "lam": row(lru_lambda[li]),
        "w_br_lru": w_br_lru[li].astype(BF16), "w_br_attn": w_br_attn[li].astype(BF16), "w_out": w_out[li].astype(BF16),
        "ln1_g": row(ln1_g[li]), "ln1_b": row(ln1_b[li]), "ln2_g": row(ln2_g[li]), "ln2_b": row(ln2_b[li]),
        "w_router": w_router, "b_router": b_router,
        "w_gate": by_group(w_gate[li]), "w_up": by_group(w_up[li]), "w_down": by_group(w_down[li]),
    }


def kernel(x_prompt, x_sample, cache_k, cache_v, cache_logf, state_conv, state_h, page_table, c_prompt, c_sample, w_mod, b_mod, w_in, b_f, conv_w, conv_b, w_rg, b_rg, w_ig, b_ig, lru_lambda, w_br_lru, w_br_attn, w_out, ln1_g, ln1_b, w_grp, b_grp, w_rt, b_rt, w_gate, w_up, w_down, ln2_g, ln2_b):
    b, t_len, d = x_prompt.shape
    db, s_len, _ = x_sample.shape
    depth = w_mod.shape[0]
    h = N_HEADS
    hd = d // h
    assert s_len == 1, "the sample group decodes one token per sequence"
    assert t_len >= CONV_W - 1
    alpha = (2.0 * depth) ** 0.25
    q_scale = hd ** -0.5 * LOG2_E

    xp = x_prompt.reshape(b * t_len, d)
    xs = x_sample.reshape(db, d)
    rc = -(-(b + db) // SUBLANES) * SUBLANES
    c_all = jnp.concatenate([c_prompt, c_sample, jnp.zeros((rc - b - db, d), F32)], axis=0)
    st_p, st_s = [], []
    for li in range(depth):
        lw = _prep_layer_weights(li, w_in, b_f, conv_w, conv_b, w_rg, b_rg, w_ig, b_ig, lru_lambda, w_br_lru,
                                 w_br_attn, w_out, ln1_g, ln1_b, w_grp, b_grp, w_rt, b_rt, w_gate, w_up, w_down,
                                 ln2_g, ln2_b)
        mods = _mods_call(c_all, w_mod[li], b_mod[li])
        mods_p = mods[:b].reshape(b, 1, 6 * d)
        mods_s = mods[b:b + db].reshape(1, db, 6 * d)

        xb, gb, qx, k, v, kx, vb, g_lru, g_att, logf, _ = _inproj_call(
            xp, mods_p, lw["w_cat"], lw["bf_pad"], per_row=False, rows_per_batch=t_len, q_scale=q_scale)
        lru, h_last = _lru_prompt_call(xb, gb, b, t_len, lw)
        att = _attn_prompt_call(qx, kx, vb, b, t_len)
        x1, u2, gates = _merge_call(xp, lru, att, g_lru, g_att, mods_p, lw, per_row=False, rows_per_batch=t_len, alpha=alpha)
        xp = _moe_call(u2, gates, x1, mods_p, lw, per_row=False, rows_per_batch=t_len, alpha=alpha)
        st_p.append((k.reshape(b, t_len, h, hd), v.reshape(b, t_len, h, hd), logf.reshape(b, t_len, h),
                     xb.reshape(b, t_len, d)[:, t_len - (CONV_W - 1):], h_last.reshape(b, d)))

        xb_s, gb_s, qx_s, k_s, v_s, _, _, gl_s, ga_s, logf_s, _ = _inproj_call(
            xs, mods_s, lw["w_cat"], lw["bf_pad"], per_row=True, rows_per_batch=1, q_scale=q_scale)
        q_s = qx_s.reshape(db, h, 2, hd)[:, :, 0, :].reshape(db, d)
        taps = [state_conv[li][:, tap, :] for tap in range(CONV_W - 1)]
        lru_s, h_new = _lru_sample_call(xb_s, gb_s, taps, state_h[li], lw)
        ct_pages = _logf_pages_call(cache_logf[li])
        att_s = _attn_sample_call(q_s, k_s, v_s, logf_s, cache_k[li], cache_v[li], ct_pages, page_table)
        x1_s, u2_s, gates_s = _merge_call(xs, lru_s, att_s.astype(BF16), gl_s, ga_s, mods_s, lw,
                                          per_row=True, rows_per_batch=1, alpha=alpha)
        xs = _moe_call(u2_s, gates_s, x1_s, mods_s, lw, per_row=True, rows_per_batch=1, alpha=alpha)
        conv_new = jnp.concatenate([state_conv[li][:, 1:], xb_s[:, None, :]], axis=1)
        st_s.append((k_s.reshape(db, 1, h, hd), v_s.reshape(db, 1, h, hd), logf_s.reshape(db, 1, h), conv_new, h_new))

    stack = lambda sts, i: jnp.stack([s[i] for s in sts])
    return (xp.reshape(b, t_len, d), xs.reshape(db, 1, d),
            stack(st_p, 0), stack(st_p, 1), stack(st_p, 2), stack(st_p, 3), stack(st_p, 4),
            stack(st_s, 0), stack(st_s, 1), stack(st_s, 2), stack(st_s, 3), stack(st_s, 4))
```
